```python
import jax, jax.numpy as jnp
from jax import lax
import numpy as np

D_MODEL = 2048
BATCH = 8
SEQ = 2048
DEPTH = 1

RET_HEADS = 4
RET_DK = 128
RET_DV = 256
RET_WIDTH = RET_HEADS * RET_DV
HG_HEADS = 8
HG_DK = 128
HG_DV = 128
HG_WIDTH = HG_HEADS * HG_DV
MIX_WIDTH = RET_WIDTH + HG_WIDTH
RET_CHUNK = 128
HG_CHUNK = 64
ROPE_BASE = 10000.0
EPS = 1e-6
ALPHA = (2.0 * DEPTH) ** 0.25
BETA = (8.0 * DEPTH) ** -0.25
RET_LOG2_DECAY = tuple(5.0 + 7.0 * h / (RET_HEADS - 1) for h in range(RET_HEADS))
SPLITS = (RET_HEADS * RET_DK, RET_HEADS * RET_DK, RET_WIDTH, RET_WIDTH,
          HG_HEADS * HG_DK, HG_HEADS * HG_DK, HG_WIDTH, HG_WIDTH)
IN_WIDTH = sum(SPLITS)
SPLIT_POINTS = [int(v) for v in np.cumsum(SPLITS)[:-1]]

kernel_name = "hymba_retnet_hgrn2_deepnorm_adaln"


def _layernorm_noaffine(x):
    xf = x.astype(jnp.float32)
    mu = jnp.mean(xf, axis=-1, keepdims=True)
    var = jnp.mean(jnp.square(xf - mu), axis=-1, keepdims=True)
    return (xf - mu) * lax.rsqrt(var + EPS)


def _rmsnorm(x):
    xf = x.astype(jnp.float32)
    return xf * lax.rsqrt(jnp.mean(jnp.square(xf), axis=-1, keepdims=True) + EPS)


def _rotary(x, positions):
    half = x.shape[-1] // 2
    freqs = ROPE_BASE ** (-jnp.arange(half, dtype=jnp.float32) / half)
    ang = positions.astype(jnp.float32)[..., None] * freqs
    cos = jnp.cos(ang)[:, :, None, :]
    sin = jnp.sin(ang)[:, :, None, :]
    x1, x2 = x[..., :half], x[..., half:]
    return jnp.concatenate([x1 * cos - x2 * sin, x1 * sin + x2 * cos], axis=-1).astype(x.dtype)


def _retention(q, k, v):
    B, T, H, DK = q.shape
    DV = v.shape[-1]
    C = RET_CHUNK
    N = T // C
    lg = jnp.log1p(-jnp.exp2(-jnp.asarray(RET_LOG2_DECAY, jnp.float32)))
    idx = jnp.arange(C, dtype=jnp.float32)
    diff = idx[:, None] - idx[None, :]
    causal = diff >= 0
    d_intra = jnp.where(causal, jnp.exp(jnp.where(causal, diff, 0.0)[None] * lg[:, None, None]), 0.0)
    qc = q.reshape(B, N, C, H, DK)
    kc = k.reshape(B, N, C, H, DK)
    vc = v.reshape(B, N, C, H, DV)
    scores = jnp.einsum('bnihd,bnjhd->bnhij', qc, kc) * d_intra
    intra = jnp.einsum('bnhij,bnjhe->bnihe', scores, vc)
    zeta = jnp.exp((C - 1 - idx)[None, :] * lg[:, None])
    kv = jnp.einsum('bnjhd,hj,bnjhe->nbhde', kc, zeta, vc)
    chunk_decay = jnp.exp(C * lg)[None, :, None, None]

    def step(state, kv_n):
        return chunk_decay * state + kv_n, state

    _, s_prev = lax.scan(step, jnp.zeros((B, H, DK, DV), kv.dtype), kv)
    xi = jnp.exp((idx + 1)[None, :] * lg[:, None])
    inter = jnp.einsum('bnihd,hi,nbhde->bnihe', qc, xi, s_prev)
    return (intra + inter).reshape(B, T, H, DV)


def _hgrn2(q, f_pre, v, lb):
    B, T, H, DK = q.shape
    DV = v.shape[-1]
    C = HG_CHUNK
    N = T // C
    fp = f_pre.astype(jnp.float32)
    log_f = jnp.log(lb + (1.0 - lb) * jax.nn.sigmoid(fp))
    k = (1.0 - lb) * jax.nn.sigmoid(-fp)

    def to_chunks(a):
        return a.reshape(B, N, C, H, a.shape[-1]).swapaxes(0, 1)

    causal = jnp.tril(jnp.ones((C, C), bool))[None, :, :, None, None]

    def step(state, xs):
        qn, kn, vn, gn = xs
        b = jnp.cumsum(gn, axis=1)
        rel = jnp.exp(jnp.where(causal, b[:, :, None] - b[:, None, :], -jnp.inf))
        attn = jnp.einsum('bthd,bshd,btshd->bhts', qn, kn, rel)
        intra = jnp.einsum('bhts,bshe->bthe', attn, vn)
        inter = jnp.einsum('bthd,bhde->bthe', qn * jnp.exp(b), state)
        b_last = b[:, -1]
        new_state = (jnp.exp(b_last)[..., None] * state
                     + jnp.einsum('bshd,bshe->bhde', kn * jnp.exp(b_last[:, None] - b), vn))
        return new_state, intra + inter

    xs = (to_chunks(q.astype(jnp.float32)), to_chunks(k), to_chunks(v.astype(jnp.float32)), to_chunks(log_f))
    _, out = lax.scan(step, jnp.zeros((B, H, DK, DV), jnp.float32), xs)
    return out.swapaxes(0, 1).reshape(B, T, H, DV)


def setup_inputs(seed: int = 0) -> dict:
    key = jax.random.key(seed)
    ks = jax.random.split(key, 12)
    f32 = jnp.float32
    x = jax.random.normal(ks[0], (BATCH, SEQ, D_MODEL), f32)
    c = jax.random.normal(ks[1], (BATCH, D_MODEL), f32)
    offset = jax.random.randint(ks[2], (BATCH, 1), 0, 1024, dtype=jnp.int32)
    positions = (offset + jnp.arange(SEQ, dtype=jnp.int32)[None, :]).astype(jnp.int32)
    w_ada = 0.5 * D_MODEL ** -0.5 * jax.random.normal(ks[3], (DEPTH, D_MODEL, 3 * D_MODEL), f32)
    b_ada = 0.01 * jax.random.normal(ks[4], (DEPTH, 3 * D_MODEL), f32)
    col_scale = jnp.asarray(np.concatenate([
        np.full(n, BETA if j in (2, 6) else 1.0, np.float32) for j, n in enumerate(SPLITS)]))
    w_in = D_MODEL ** -0.5 * jax.random.normal(ks[5], (DEPTH, D_MODEL, IN_WIDTH), f32) * col_scale
    lb_logits = 0.1 * jax.random.normal(ks[6], (DEPTH + 1, HG_HEADS * HG_DK), f32)
    hg_norm_w = 1.0 + 0.02 * jax.random.normal(ks[7], (DEPTH, HG_WIDTH), f32)
    w_out = BETA * MIX_WIDTH ** -0.5 * jax.random.normal(ks[8], (DEPTH, MIX_WIDTH, D_MODEL), f32)
    ln_w = 1.0 + 0.02 * jax.random.normal(ks[9], (DEPTH, D_MODEL), f32)
    ln_b = 0.02 * jax.random.normal(ks[10], (DEPTH, D_MODEL), f32)
    return {"x": x, "c": c, "positions": positions, "w_ada": w_ada, "b_ada": b_ada,
            "w_in": w_in, "lb_logits": lb_logits, "hg_norm_w": hg_norm_w, "w_out": w_out,
            "ln_w": ln_w, "ln_b": ln_b}


def reference(x, c, positions, w_ada, b_ada, w_in, lb_logits, hg_norm_w, w_out, ln_w, ln_b):
    B, T, _ = x.shape
    lb_all = jnp.cumsum(jax.nn.softmax(lb_logits.astype(jnp.float32), axis=0), axis=0)
    cond = jax.nn.silu(c)
    for l in range(DEPTH):
        mod = cond @ w_ada[l] + b_ada[l]
        shift, scale, gate = jnp.split(mod, 3, axis=-1)
        h = (_layernorm_noaffine(x) * (1.0 + scale[:, None]) + shift[:, None]).astype(x.dtype)
        proj = h @ w_in[l]
        rq, rk, rv, rg, hq, hf, hi, hg = jnp.split(proj, SPLIT_POINTS, axis=-1)
        rq = _rotary(rq.reshape(B, T, RET_HEADS, RET_DK), positions) * (RET_DK ** -0.5)
        rk = _rotary(rk.reshape(B, T, RET_HEADS, RET_DK), positions)
        ret = _retention(rq, rk, rv.reshape(B, T, RET_HEADS, RET_DV))
        ret = _rmsnorm(ret).reshape(B, T, RET_WIDTH) * jax.nn.silu(rg.astype(jnp.float32))
        lb = lb_all[l].reshape(HG_HEADS, HG_DK)
        hgo = _hgrn2(hq.reshape(B, T, HG_HEADS, HG_DK), hf.reshape(B, T, HG_HEADS, HG_DK),
                     hi.reshape(B, T, HG_HEADS, HG_DV), lb)
        hgo = (_rmsnorm(hgo) * hg_norm_w[l].reshape(HG_HEADS, HG_DV)).reshape(B, T, HG_WIDTH)
        hgo = hgo * jax.nn.silu(hg.astype(jnp.float32))
        mixed = jnp.concatenate([ret, hgo], axis=-1).astype(x.dtype)
        y = mixed @ w_out[l]
        z = ALPHA * x + gate[:, None] * y
        x = (_layernorm_noaffine(z) * ln_w[l] + ln_b[l]).astype(x.dtype)
    return x
```

```python
import functools

import numpy as np
import jax
import jax.numpy as jnp
from jax import lax
from jax.experimental import pallas as pl
from jax.experimental.pallas import tpu as pltpu

RET_HEADS = 4
RET_DK = 128
RET_DV = 256
HG_HEADS = 8
HG_DK = 128
HG_DV = 128
RET_CHUNK = 128
HG_CHUNK = 64
ROPE_BASE = 10000.0
EPS = 1e-6
DEPTH = 1
ALPHA = (2.0 * DEPTH) ** 0.25
RET_LOG2_DECAY = tuple(5.0 + 7.0 * h / (RET_HEADS - 1) for h in range(RET_HEADS))

RET_QK = RET_HEADS * RET_DK
RET_WIDTH = RET_HEADS * RET_DV
HG_QK = HG_HEADS * HG_DK
HG_WIDTH = HG_HEADS * HG_DV
MIX_WIDTH = RET_WIDTH + HG_WIDTH
OFF_RQ = 0
OFF_RK = OFF_RQ + RET_QK
OFF_RV = OFF_RK + RET_QK
OFF_RG = OFF_RV + RET_WIDTH
OFF_HQ = OFF_RG + RET_WIDTH
OFF_HF = OFF_HQ + HG_QK
OFF_HI = OFF_HF + HG_QK
OFF_HG = OFF_HI + HG_WIDTH
IN_WIDTH = OFF_HG + HG_WIDTH

F32 = jnp.float32
BF16 = jnp.bfloat16
VMEM_LIMIT_BYTES = 56 * 1024 * 1024


def _sigmoid(x):
    return 1.0 / (1.0 + jnp.exp(-x))


def _ada_kernel(c_ref, w_ref, b_ref, o_ref):
    c = c_ref[...]
    cond = c * _sigmoid(c)
    o_ref[...] = jnp.dot(cond, w_ref[...], preferred_element_type=F32) + b_ref[...]


def _ada_call(c, w, b):
    bsz, d = c.shape
    n = w.shape[1]
    tn = 512 if n % 512 == 0 else n
    return pl.pallas_call(
        _ada_kernel,
        grid=(n // tn,),
        in_specs=[pl.BlockSpec((bsz, d), lambda j: (0, 0)),
                  pl.BlockSpec((d, tn), lambda j: (0, j)),
                  pl.BlockSpec((1, tn), lambda j: (0, j))],
        out_specs=pl.BlockSpec((bsz, tn), lambda j: (0, j)),
        out_shape=jax.ShapeDtypeStruct((bsz, n), F32),
        compiler_params=pltpu.CompilerParams(dimension_semantics=("arbitrary",),
                                             vmem_limit_bytes=VMEM_LIMIT_BYTES),
        name="ada_mod",
    )(c, w, b.reshape(1, n))


def _rope_kernel(pos_ref, freq_ref, sign_ref, cos_ref, sin_ref):
    ang = pos_ref[0].astype(F32) * freq_ref[...]
    cos_ref[0] = jnp.cos(ang)
    sin_ref[0] = jnp.sin(ang) * sign_ref[...]


def _rope_call(positions):
    bsz, t = positions.shape
    half = RET_DK // 2
    freqs = ROPE_BASE ** (-jnp.arange(half, dtype=F32) / half)
    freq2 = jnp.concatenate([freqs, freqs]).reshape(1, RET_DK)
    sign = jnp.concatenate([-jnp.ones((half,), F32), jnp.ones((half,), F32)]).reshape(1, RET_DK)
    tt = 512 if t % 512 == 0 else t
    shp = jax.ShapeDtypeStruct((bsz, t, RET_DK), F32)
    return pl.pallas_call(
        _rope_kernel,
        grid=(bsz, t // tt),
        in_specs=[pl.BlockSpec((1, tt, 1), lambda b, i: (b, i, 0)),
                  pl.BlockSpec((1, RET_DK), lambda b, i: (0, 0)),
                  pl.BlockSpec((1, RET_DK), lambda b, i: (0, 0))],
        out_specs=[pl.BlockSpec((1, tt, RET_DK), lambda b, i: (b, i, 0)),
                   pl.BlockSpec((1, tt, RET_DK), lambda b, i: (b, i, 0))],
        out_shape=[shp, shp],
        compiler_params=pltpu.CompilerParams(dimension_semantics=("arbitrary", "arbitrary")),
        name="rope_tables",
    )(positions.reshape(bsz, t, 1), freq2, sign)


def _proj_kernel(x_ref, shift_ref, scale_ref, w_ref, o_ref, h_ref):
    @pl.when(pl.program_id(1) == 0)
    def _():
        x = x_ref[...]
        mu = jnp.mean(x, axis=-1, keepdims=True)
        xc = x - mu
        var = jnp.mean(xc * xc, axis=-1, keepdims=True)
        h = xc * lax.rsqrt(var + EPS) * (1.0 + scale_ref[0]) + shift_ref[0]
        h_ref[...] = h.astype(BF16)

    o_ref[...] = jnp.dot(h_ref[...], w_ref[...], preferred_element_type=F32)


def _proj_call(x2d, mod3, w_bf16, seq):
    m, d = x2d.shape
    n = w_bf16.shape[1]
    tm = min(1024, seq)
    tn = 1024
    steps_per_batch = seq // tm
    return pl.pallas_call(
        _proj_kernel,
        grid=(m // tm, n // tn),
        in_specs=[pl.BlockSpec((tm, d), lambda i, j: (i, 0)),
                  pl.BlockSpec((1, 1, d), lambda i, j: (i // steps_per_batch, 0, 0)),
                  pl.BlockSpec((1, 1, d), lambda i, j: (i // steps_per_batch, 0, 1)),
                  pl.BlockSpec((d, tn), lambda i, j: (0, j))],
        out_specs=pl.BlockSpec((tm, tn), lambda i, j: (i, j)),
        out_shape=jax.ShapeDtypeStruct((m, n), F32),
        scratch_shapes=[pltpu.VMEM((tm, d), BF16)],
        compiler_params=pltpu.CompilerParams(dimension_semantics=("arbitrary", "arbitrary"),
                                             vmem_limit_bytes=VMEM_LIMIT_BYTES),
        name="ln_mod_proj",
    )(x2d, mod3, mod3, w_bf16)


def _dot_nt(a, b):
    return lax.dot_general(a, b, (((1,), (1,)), ((), ())), preferred_element_type=F32)


def _dot_tn(a, b):
    return lax.dot_general(a, b, (((0,), (0,)), ((), ())), preferred_element_type=F32)


def _split3(x):
    hi = x.astype(BF16)
    r1 = x - hi.astype(F32)
    mid = r1.astype(BF16)
    lo = (r1 - mid.astype(F32)).astype(BF16)
    return hi, mid, lo


def _mix_kernel(proj_ref, cos_ref, sin_ref, x_ref, gate_ref, wout_ref, lnw_ref, lnb_ref,
                lbl_ref, hgw_ref, dmask_ref, zeta_ref, xi_ref, cdec_ref,
                o_ref, sret_ref, shg_ref, mixed_ref, *, tm):
    @pl.when(pl.program_id(1) == 0)
    def _():
        sret_ref[...] = jnp.zeros_like(sret_ref)
        shg_ref[...] = jnp.zeros_like(shg_ref)

    for c in range(tm // RET_CHUNK):
        rows = pl.ds(c * RET_CHUNK, RET_CHUNK)
        cosf = cos_ref[0, rows, :]
        sinf = sin_ref[0, rows, :]
        for h in range(RET_HEADS):
            q = proj_ref[rows, pl.ds(OFF_RQ + h * RET_DK, RET_DK)]
            k = proj_ref[rows, pl.ds(OFF_RK + h * RET_DK, RET_DK)]
            v = proj_ref[rows, pl.ds(OFF_RV + h * RET_DV, RET_DV)].astype(BF16)
            q = (q * cosf + pltpu.roll(q, RET_DK // 2, 1) * sinf) * (RET_DK ** -0.5)
            k = k * cosf + pltpu.roll(k, RET_DK // 2, 1) * sinf
            qb = q.astype(BF16)
            scores = _dot_nt(qb, k.astype(BF16)) * dmask_ref[h]
            intra = jnp.dot(scores.astype(BF16), v, preferred_element_type=F32)
            s_prev = sret_ref[h]
            inter = jnp.dot((q * xi_ref[h]).astype(BF16), s_prev.astype(BF16),
                            preferred_element_type=F32)
            kv = _dot_tn((k * zeta_ref[h]).astype(BF16), v)
            sret_ref[h] = cdec_ref[h] * s_prev + kv
            ret = intra + inter
            ret = ret * lax.rsqrt(jnp.mean(ret * ret, axis=-1, keepdims=True) + EPS)
            g = proj_ref[rows, pl.ds(OFF_RG + h * RET_DV, RET_DV)]
            mixed_ref[rows, pl.ds(h * RET_DV, RET_DV)] = (ret * (g * _sigmoid(g))).astype(BF16)

    l0 = lbl_ref[0:1, :]
    l1 = lbl_ref[1:2, :]
    lmax = jnp.maximum(l0, l1)
    e0 = jnp.exp(l0 - lmax)
    lb = e0 / (e0 + jnp.exp(l1 - lmax))
    row = lax.broadcasted_iota(jnp.int32, (HG_CHUNK, HG_CHUNK), 0)
    col = lax.broadcasted_iota(jnp.int32, (HG_CHUNK, HG_CHUNK), 1)
    causal = row >= col
    tri = causal.astype(BF16)
    for c in range(tm // HG_CHUNK):
        rows = pl.ds(c * HG_CHUNK, HG_CHUNK)
        fp = proj_ref[rows, pl.ds(OFF_HF, HG_QK)]
        sg = _sigmoid(fp)
        logf = jnp.log(lb + (1.0 - lb) * sg)
        kk = (1.0 - lb) * (1.0 - sg)
        hi, mid, lo = _split3(logf)
        b = (jnp.dot(tri, hi, preferred_element_type=F32)
             + jnp.dot(tri, mid, preferred_element_type=F32)
             + jnp.dot(tri, lo, preferred_element_type=F32))
        b_last = b[HG_CHUNK - 1:HG_CHUNK, :]
        hq = proj_ref[rows, pl.ds(OFF_HQ, HG_QK)]
        qd = hq * jnp.exp(b)
        kd = kk * jnp.exp(-b)
        ks = kk * jnp.exp(b_last - b)
        e_last = jnp.exp(b_last)
        for h in range(HG_HEADS):
            cols = slice(h * HG_DK, (h + 1) * HG_DK)
            v = proj_ref[rows, pl.ds(OFF_HI + h * HG_DV, HG_DV)].astype(BF16)
            qh = qd[:, cols].astype(BF16)
            attn = jnp.where(causal, _dot_nt(qh, kd[:, cols].astype(BF16)), 0.0)
            st_prev = shg_ref[h]
            o = (jnp.dot(attn.astype(BF16), v, preferred_element_type=F32)
                 + _dot_nt(qh, st_prev.astype(BF16)))
            shg_ref[h] = st_prev * e_last[:, cols] + _dot_tn(v, ks[:, cols].astype(BF16))
            o = o * lax.rsqrt(jnp.mean(o * o, axis=-1, keepdims=True) + EPS)
            o = o * hgw_ref[:, pl.ds(h * HG_DV, HG_DV)]
            g = proj_ref[rows, pl.ds(OFF_HG + h * HG_DV, HG_DV)]
            mixed_ref[rows, pl.ds(RET_WIDTH + h * HG_DV, HG_DV)] = (o * (g * _sigmoid(g))).astype(BF16)

    y = jnp.dot(mixed_ref[...], wout_ref[...], preferred_element_type=F32)
    z = ALPHA * x_ref[...] + gate_ref[0] * y
    mu = jnp.mean(z, axis=-1, keepdims=True)
    zc = z - mu
    var = jnp.mean(zc * zc, axis=-1, keepdims=True)
    o_ref[...] = zc * lax.rsqrt(var + EPS) * lnw_ref[...] + lnb_ref[...]


def _ret_constants():
    lg = np.log1p(-np.exp2(-np.asarray(RET_LOG2_DECAY, np.float64)))
    idx = np.arange(RET_CHUNK, dtype=np.float64)
    diff = idx[:, None] - idx[None, :]
    dmask = np.where(diff >= 0, np.exp(np.where(diff >= 0, diff, 0.0)[None] * lg[:, None, None]), 0.0)
    zeta = np.exp((RET_CHUNK - 1 - idx)[None, :] * lg[:, None])
    xi = np.exp((idx + 1)[None, :] * lg[:, None])
    cdec = np.exp(RET_CHUNK * lg)
    bc = lambda a: np.broadcast_to(a[:, :, None], (RET_HEADS, RET_CHUNK, RET_DK))
    return (jnp.asarray(dmask, F32), jnp.asarray(bc(zeta), F32), jnp.asarray(bc(xi), F32),
            jnp.asarray(np.broadcast_to(cdec[:, None, None], (RET_HEADS, RET_DK, RET_DV)), F32))


def _mix_call(proj, cosf, sinf, x2d, mod3, w_out_bf16, ln_w, ln_b, lb_logits, hg_norm_w, seq):
    m, d = x2d.shape
    tm = min(256, seq)
    steps = seq // tm
    bsz = m // seq
    dmask, zeta, xi, cdec = _ret_constants()
    full = lambda a: pl.BlockSpec(a.shape, lambda b, t: (0,) * a.ndim)
    return pl.pallas_call(
        functools.partial(_mix_kernel, tm=tm),
        grid=(bsz, steps),
        in_specs=[pl.BlockSpec((tm, IN_WIDTH), lambda b, t: (b * steps + t, 0)),
                  pl.BlockSpec((1, tm, RET_DK), lambda b, t: (b, t, 0)),
                  pl.BlockSpec((1, tm, RET_DK), lambda b, t: (b, t, 0)),
                  pl.BlockSpec((tm, d), lambda b, t: (b * steps + t, 0)),
                  pl.BlockSpec((1, 1, d), lambda b, t: (b, 0, 2)),
                  full(w_out_bf16), full(ln_w), full(ln_b), full(lb_logits), full(hg_norm_w),
                  full(dmask), full(zeta), full(xi), full(cdec)],
        out_specs=pl.BlockSpec((tm, d), lambda b, t: (b * steps + t, 0)),
        out_shape=jax.ShapeDtypeStruct((m, d), F32),
        scratch_shapes=[pltpu.VMEM((RET_HEADS, RET_DK, RET_DV), F32),
                        pltpu.VMEM((HG_HEADS, HG_DV, HG_DK), F32),
                        pltpu.VMEM((tm, MIX_WIDTH), BF16)],
        compiler_params=pltpu.CompilerParams(dimension_semantics=("arbitrary", "arbitrary"),
                                             vmem_limit_bytes=VMEM_LIMIT_BYTES),
        name="mix_out",
    )(proj, cosf, sinf, x2d, mod3, w_out_bf16, ln_w, ln_b, lb_logits, hg_norm_w,
      dmask, zeta, xi, cdec)


def kernel(x, c, positions, w_ada, b_ada, w_in, lb_logits, hg_norm_w, w_out, ln_w, ln_b):
    bsz, seq, d = x.shape
    assert w_ada.shape[0] == DEPTH and seq % RET_CHUNK == 0 and d % 128 == 0
    mod = _ada_call(c, w_ada[0], b_ada[0])
    mod3 = mod.reshape(bsz, 1, 3 * d)
    cosf, sinf = _rope_call(positions)
    x2d = x.reshape(bsz * seq, d)
    proj = _proj_call(x2d, mod3, w_in[0].astype(BF16), seq)
    out = _mix_call(proj, cosf, sinf, x2d, mod3, w_out[0].astype(BF16),
                    ln_w[0].reshape(1, d), ln_b[0].reshape(1, d),
                    lb_logits.astype(F32), hg_norm_w[0].reshape(1, HG_WIDTH), seq)
    return out.reshape(bsz, seq, d)
```

```python
import functools

import numpy as np
import jax
import jax.numpy as jnp
from jax import lax
from jax.experimental import pallas as pl
from jax.experimental.pallas import tpu as pltpu

RET_HEADS = 4
RET_DK = 128
RET_DV = 256
HG_HEADS = 8
HG_DK = 128
HG_DV = 128
RET_CHUNK = 128
HG_CHUNK = 64
ROPE_BASE = 10000.0
EPS = 1e-6
DEPTH = 1
ALPHA = (2.0 * DEPTH) ** 0.25
RET_LOG2_DECAY = tuple(5.0 + 7.0 * h / (RET_HEADS - 1) for h in range(RET_HEADS))
RET_LOG_DECAY = tuple(float(np.log1p(-np.exp2(-e))) for e in RET_LOG2_DECAY)

RET_QK = RET_HEADS * RET_DK
RET_WIDTH = RET_HEADS * RET_DV
HG_QK = HG_HEADS * HG_DK
HG_WIDTH = HG_HEADS * HG_DV
MIX_WIDTH = RET_WIDTH + HG_WIDTH
OFF_RQ = 0
OFF_RK = OFF_RQ + RET_QK
OFF_RV = OFF_RK + RET_QK
OFF_RG = OFF_RV + RET_WIDTH
OFF_HQ = OFF_RG + RET_WIDTH
OFF_HF = OFF_HQ + HG_QK
OFF_HI = OFF_HF + HG_QK
OFF_HG = OFF_HI + HG_WIDTH
IN_WIDTH = OFF_HG + HG_WIDTH
RET_GROUP = 2 * RET_DK + 2 * RET_DV
HG_GROUP = 2 * HG_DK + 2 * HG_DV
HG_BASE = RET_HEADS * RET_GROUP

F32 = jnp.float32
BF16 = jnp.bfloat16
TOKEN_TILE = 256
PIECE = 256
V7X_VMEM_BYTES = 64 * 1024 * 1024
VMEM_LIMIT_BYTES = V7X_VMEM_BYTES - 2 * 1024 * 1024
HEAD_ORDER = (("ret", 0), ("hg", 0), ("hg", 1), ("ret", 1), ("hg", 2), ("hg", 3),
              ("ret", 2), ("hg", 4), ("hg", 5), ("ret", 3), ("hg", 6), ("hg", 7))


def _head_major_columns():
    idx = []
    for h in range(RET_HEADS):
        idx += list(range(OFF_RQ + h * RET_DK, OFF_RQ + (h + 1) * RET_DK))
        idx += list(range(OFF_RK + h * RET_DK, OFF_RK + (h + 1) * RET_DK))
        idx += list(range(OFF_RV + h * RET_DV, OFF_RV + (h + 1) * RET_DV))
        idx += list(range(OFF_RG + h * RET_DV, OFF_RG + (h + 1) * RET_DV))
    for h in range(HG_HEADS):
        idx += list(range(OFF_HQ + h * HG_DK, OFF_HQ + (h + 1) * HG_DK))
        idx += list(range(OFF_HF + h * HG_DK, OFF_HF + (h + 1) * HG_DK))
        idx += list(range(OFF_HI + h * HG_DV, OFF_HI + (h + 1) * HG_DV))
        idx += list(range(OFF_HG + h * HG_DV, OFF_HG + (h + 1) * HG_DV))
    return np.asarray(idx, np.int32)


def _sigmoid(x):
    return 1.0 / (1.0 + jnp.exp(-x))


def _ada_kernel(c_ref, w_ref, b_ref, o_ref):
    c = c_ref[...]
    cond = c * _sigmoid(c)
    o_ref[...] = jnp.dot(cond, w_ref[...], preferred_element_type=F32) + b_ref[...]


def _ada_call(c, w, b):
    bsz, d = c.shape
    n = w.shape[1]
    tn = 512 if n % 512 == 0 else n
    return pl.pallas_call(
        _ada_kernel,
        grid=(n // tn,),
        in_specs=[pl.BlockSpec((bsz, d), lambda j: (0, 0)),
                  pl.BlockSpec((d, tn), lambda j: (0, j)),
                  pl.BlockSpec((1, tn), lambda j: (0, j))],
        out_specs=pl.BlockSpec((bsz, tn), lambda j: (0, j)),
        out_shape=jax.ShapeDtypeStruct((bsz, n), F32),
        compiler_params=pltpu.CompilerParams(dimension_semantics=("arbitrary",)),
        name="ada_mod",
    )(c, w, b.reshape(1, n))


def _dot(a, b):
    return jnp.dot(a, b, preferred_element_type=F32)


def _dot_nt(a, b):
    return lax.dot_general(a, b, (((1,), (1,)), ((), ())), preferred_element_type=F32)


def _dot_tn(a, b):
    return lax.dot_general(a, b, (((0,), (0,)), ((), ())), preferred_element_type=F32)


def _split3(x):
    hi = x.astype(BF16)
    r1 = x - hi.astype(F32)
    mid = r1.astype(BF16)
    lo = (r1 - mid.astype(F32)).astype(BF16)
    return hi, mid, lo


def _layernorm(x):
    mu = jnp.mean(x, axis=-1, keepdims=True)
    xc = x - mu
    var = jnp.mean(xc * xc, axis=-1, keepdims=True)
    return xc * lax.rsqrt(var + EPS)


def _rmsnorm(x):
    return x * lax.rsqrt(jnp.mean(x * x, axis=-1, keepdims=True) + EPS)


def _fused_kernel(x_ref, pos_ref, shift_ref, scale_ref, gate_ref, win_ref, wout_ref, lnw_ref, lnb_ref,
                  lbl_ref, hgw_ref, dmask_ref, zeta_ref, xi_ref, tribd_ref, freq_ref, sign_ref,
                  o_ref, pret_ref, phg_ref, sret_ref, shg_ref, mixed_ref, xprev_ref, h_ref,
                  *, tm, steps_per_batch):
    s = pl.program_id(0)
    n_ret_chunks = tm // RET_CHUNK
    n_hg_chunks = tm // HG_CHUNK

    @pl.when(s == 0)
    def _():
        pret_ref[...] = jnp.zeros_like(pret_ref)
        phg_ref[...] = jnp.zeros_like(phg_ref)
        xprev_ref[...] = jnp.zeros_like(xprev_ref)

    @pl.when(lax.rem(jnp.maximum(s - 1, 0), steps_per_batch) == 0)
    def _():
        sret_ref[...] = jnp.zeros_like(sret_ref)
        shg_ref[...] = jnp.zeros_like(shg_ref)

    pending = []

    def fill(n=1):
        for _ in range(n):
            if pending:
                pending.pop(0)()

    def proj_piece(dst_ref, h, gbase, c0):
        def run():
            dst_ref[h, :, pl.ds(c0, PIECE)] = _dot(h_ref[...], win_ref[:, pl.ds(gbase + c0, PIECE)])
        return run

    h_ref[...] = (_layernorm(x_ref[...]) * (1.0 + scale_ref[0]) + shift_ref[0]).astype(BF16)

    ang = pos_ref[0].astype(F32) * freq_ref[...]
    cosf = jnp.cos(ang)
    sinf = jnp.sin(ang) * sign_ref[...]

    def mix_ret(h):
        cdec = float(np.exp(RET_CHUNK * RET_LOG_DECAY[h]))
        scores, kv, qx, vs = [], [], [], []
        for c in range(n_ret_chunks):
            rows = pl.ds(c * RET_CHUNK, RET_CHUNK)
            cf = cosf[c * RET_CHUNK:(c + 1) * RET_CHUNK]
            sf = sinf[c * RET_CHUNK:(c + 1) * RET_CHUNK]
            q = pret_ref[h, rows, pl.ds(0, RET_DK)]
            k = pret_ref[h, rows, pl.ds(RET_DK, RET_DK)]
            v = pret_ref[h, rows, pl.ds(2 * RET_DK, RET_DV)].astype(BF16)
            q = (q * cf + pltpu.roll(q, RET_DK // 2, 1) * sf) * (RET_DK ** -0.5)
            k = k * cf + pltpu.roll(k, RET_DK // 2, 1) * sf
            scores.append(_dot_nt(q.astype(BF16), k.astype(BF16)))
            kv.append(_dot_tn((k * zeta_ref[h]).astype(BF16), v))
            qx.append((q * xi_ref[h]).astype(BF16))
            vs.append(v)
        fill()
        state = sret_ref[h]
        rets = []
        for c in range(n_ret_chunks):
            p = (scores[c] * dmask_ref[h]).astype(BF16)
            rets.append(_dot(p, vs[c]) + _dot(qx[c], state.astype(BF16)))
            state = cdec * state + kv[c]
        sret_ref[h] = state
        fill()
        for c in range(n_ret_chunks):
            rows = pl.ds(c * RET_CHUNK, RET_CHUNK)
            g = pret_ref[h, rows, pl.ds(2 * RET_DK + RET_DV, RET_DV)]
            mixed_ref[rows, pl.ds(h * RET_DV, RET_DV)] = (_rmsnorm(rets[c]) * (g * _sigmoid(g))).astype(BF16)
        fill()
        for c0 in range(0, RET_GROUP, PIECE):
            pending.append(proj_piece(pret_ref, h, h * RET_GROUP, c0))

    l0 = lbl_ref[0:1, :]
    l1 = lbl_ref[1:2, :]
    lmax = jnp.maximum(l0, l1)
    e0 = jnp.exp(l0 - lmax)
    lb_all = e0 / (e0 + jnp.exp(l1 - lmax))
    row = lax.broadcasted_iota(jnp.int32, (HG_CHUNK, HG_CHUNK), 0)
    col = lax.broadcasted_iota(jnp.int32, (HG_CHUNK, HG_CHUNK), 1)
    causal = row >= col

    def mix_hg(h):
        lb = lb_all[:, h * HG_DK:(h + 1) * HG_DK]
        fp = phg_ref[h, :, pl.ds(HG_DK, HG_DK)]
        sg = _sigmoid(fp)
        logf = jnp.log(lb + (1.0 - lb) * sg)
        kk = (1.0 - lb) * (1.0 - sg)
        parts = _dot(tribd_ref[...], jnp.concatenate(_split3(logf), axis=1))
        b = parts[:, 0:HG_DK] + parts[:, HG_DK:2 * HG_DK] + parts[:, 2 * HG_DK:3 * HG_DK]
        fill()
        hq = phg_ref[h, :, pl.ds(0, HG_DK)]
        qd = (hq * jnp.exp(b)).astype(BF16)
        kd = (kk * jnp.exp(-b)).astype(BF16)
        attn, kvt, elast, vs = [], [], [], []
        for c in range(n_hg_chunks):
            r0, r1 = c * HG_CHUNK, (c + 1) * HG_CHUNK
            b_last = b[r1 - 1:r1, :]
            ks = (kk[r0:r1] * jnp.exp(b_last - b[r0:r1])).astype(BF16)
            v = phg_ref[h, pl.ds(r0, HG_CHUNK), pl.ds(2 * HG_DK, HG_DV)].astype(BF16)
            attn.append(_dot_nt(qd[r0:r1], kd[r0:r1]))
            kvt.append(_dot_tn(v, ks))
            elast.append(jnp.exp(b_last))
            vs.append(v)
        fill()
        state_t = shg_ref[h]
        outs = []
        for c in range(n_hg_chunks):
            r0, r1 = c * HG_CHUNK, (c + 1) * HG_CHUNK
            a = jnp.where(causal, attn[c], 0.0).astype(BF16)
            outs.append(_dot(a, vs[c]) + _dot_nt(qd[r0:r1], state_t.astype(BF16)))
            state_t = state_t * elast[c] + kvt[c]
        shg_ref[h] = state_t
        fill()
        hgw = hgw_ref[:, pl.ds(h * HG_DV, HG_DV)]
        for c in range(n_hg_chunks):
            rows = pl.ds(c * HG_CHUNK, HG_CHUNK)
            g = phg_ref[h, rows, pl.ds(2 * HG_DK + HG_DV, HG_DV)]
            mixed_ref[rows, pl.ds(RET_WIDTH + h * HG_DV, HG_DV)] = (
                _rmsnorm(outs[c]) * hgw * (g * _sigmoid(g))).astype(BF16)
        fill()
        for c0 in range(0, HG_GROUP, PIECE):
            pending.append(proj_piece(phg_ref, h, HG_BASE + h * HG_GROUP, c0))

    for kind, h in HEAD_ORDER:
        if kind == "ret":
            mix_ret(h)
        else:
            mix_hg(h)
    fill(len(pending))

    y = _dot(mixed_ref[...], wout_ref[...])
    z = ALPHA * xprev_ref[...] + gate_ref[0] * y
    o_ref[...] = _layernorm(z) * lnw_ref[...] + lnb_ref[...]
    xprev_ref[...] = x_ref[...]


def _ret_constants():
    lg = np.asarray(RET_LOG_DECAY, np.float64)
    idx = np.arange(RET_CHUNK, dtype=np.float64)
    diff = idx[:, None] - idx[None, :]
    dmask = np.where(diff >= 0, np.exp(np.where(diff >= 0, diff, 0.0)[None] * lg[:, None, None]), 0.0)
    zeta = np.exp((RET_CHUNK - 1 - idx)[None, :] * lg[:, None])
    xi = np.exp((idx + 1)[None, :] * lg[:, None])
    bc = lambda a: np.broadcast_to(a[:, :, None], (RET_HEADS, RET_CHUNK, RET_DK))
    return jnp.asarray(dmask, F32), jnp.asarray(bc(zeta), F32), jnp.asarray(bc(xi), F32)


def _chunk_tril(tm):
    i = np.arange(tm)
    m = (i[:, None] >= i[None, :]) & (i[:, None] // HG_CHUNK == i[None, :] // HG_CHUNK)
    return jnp.asarray(m, BF16)


def _fused_call(x2d, pos3, mod3, w_in_hm, w_out_bf16, ln_w, ln_b, lb_logits, hg_norm_w, seq):
    m, d = x2d.shape
    tm = min(TOKEN_TILE, seq)
    steps_per_batch = seq // tm
    n_tiles = m // tm
    dmask, zeta, xi = _ret_constants()
    tribd = _chunk_tril(tm)
    half = RET_DK // 2
    freqs = ROPE_BASE ** (-jnp.arange(half, dtype=F32) / half)
    freq2 = jnp.concatenate([freqs, freqs]).reshape(1, RET_DK)
    sign = jnp.concatenate([-jnp.ones((half,), F32), jnp.ones((half,), F32)]).reshape(1, RET_DK)

    cur = lambda s: jnp.minimum(s, n_tiles - 1)
    prv = lambda s: jnp.maximum(s - 1, 0)
    resident = lambda a: pl.BlockSpec(a.shape, lambda s: (0,) * a.ndim, pipeline_mode=pl.Buffered(1))
    return pl.pallas_call(
        functools.partial(_fused_kernel, tm=tm, steps_per_batch=steps_per_batch),
        grid=(n_tiles + 1,),
        in_specs=[pl.BlockSpec((tm, d), lambda s: (cur(s), 0)),
                  pl.BlockSpec((1, tm, 1), lambda s: (prv(s) // steps_per_batch, prv(s) % steps_per_batch, 0)),
                  pl.BlockSpec((1, 1, d), lambda s: (cur(s) // steps_per_batch, 0, 0)),
                  pl.BlockSpec((1, 1, d), lambda s: (cur(s) // steps_per_batch, 0, 1)),
                  pl.BlockSpec((1, 1, d), lambda s: (prv(s) // steps_per_batch, 0, 2)),
                  resident(w_in_hm), resident(w_out_bf16), resident(ln_w), resident(ln_b),
                  resident(lb_logits), resident(hg_norm_w), resident(dmask), resident(zeta),
                  resident(xi), resident(tribd), resident(freq2), resident(sign)],
        out_specs=pl.BlockSpec((tm, d), lambda s: (prv(s), 0)),
        out_shape=jax.ShapeDtypeStruct((m, d), F32),
        scratch_shapes=[pltpu.VMEM((RET_HEADS, tm, RET_GROUP), F32),
                        pltpu.VMEM((HG_HEADS, tm, HG_GROUP), F32),
                        pltpu.VMEM((RET_HEADS, RET_DK, RET_DV), F32),
                        pltpu.VMEM((HG_HEADS, HG_DV, HG_DK), F32),
                        pltpu.VMEM((tm, MIX_WIDTH), BF16),
                        pltpu.VMEM((tm, d), F32),
                        pltpu.VMEM((tm, d), BF16)],
        compiler_params=pltpu.CompilerParams(dimension_semantics=("arbitrary",),
                                             vmem_limit_bytes=VMEM_LIMIT_BYTES),
        name="fused_layer",
    )(x2d, pos3, mod3, mod3, mod3, w_in_hm, w_out_bf16, ln_w, ln_b, lb_logits, hg_norm_w,
      dmask, zeta, xi, tribd, freq2, sign)


def kernel(x, c, positions, w_ada, b_ada, w_in, lb_logits, hg_norm_w, w_out, ln_w, ln_b):
    bsz, seq, d = x.shape
    assert w_ada.shape[0] == DEPTH and seq % RET_CHUNK == 0 and d % 128 == 0
    mod = _ada_call(c, w_ada[0], b_ada[0])
    mod3 = mod.reshape(bsz, 1, 3 * d)
    w_in_hm = w_in[0][:, _head_major_columns()].astype(BF16)
    out = _fused_call(x.reshape(bsz * seq, d), positions.reshape(bsz, seq, 1), mod3, w_in_hm,
                      w_out[0].astype(BF16), ln_w[0].reshape(1, d), ln_b[0].reshape(1, d),
                      lb_logits.astype(F32), hg_norm_w[0].reshape(1, HG_WIDTH), seq)
    return out.reshape(bsz, seq, d)
```

```python
import functools

import numpy as np
import jax
import jax.numpy as jnp
from jax import lax
from jax.experimental import pallas as pl
from jax.experimental.pallas import tpu as pltpu

RET_HEADS = 4
RET_DK = 128
RET_DV = 256
HG_HEADS = 8
HG_DK = 128
HG_DV = 128
RET_CHUNK = 128
HG_CHUNK = 64
ROPE_BASE = 10000.0
EPS = 1e-6
DEPTH = 1
ALPHA = (2.0 * DEPTH) ** 0.25
RET_LOG2_DECAY = tuple(5.0 + 7.0 * h / (RET_HEADS - 1) for h in range(RET_HEADS))
RET_LOG_DECAY = tuple(float(np.log1p(-np.exp2(-e))) for e in RET_LOG2_DECAY)

RET_QK = RET_HEADS * RET_DK
RET_WIDTH = RET_HEADS * RET_DV
HG_QK = HG_HEADS * HG_DK
HG_WIDTH = HG_HEADS * HG_DV
MIX_WIDTH = RET_WIDTH + HG_WIDTH
OFF_RQ = 0
OFF_RK = OFF_RQ + RET_QK
OFF_RV = OFF_RK + RET_QK
OFF_RG = OFF_RV + RET_WIDTH
OFF_HQ = OFF_RG + RET_WIDTH
OFF_HF = OFF_HQ + HG_QK
OFF_HI = OFF_HF + HG_QK
OFF_HG = OFF_HI + HG_WIDTH
IN_WIDTH = OFF_HG + HG_WIDTH
RET_GROUP = 2 * RET_DK + 2 * RET_DV
HG_GROUP = 2 * HG_DK + 2 * HG_DV
HG_BASE = RET_HEADS * RET_GROUP

F32 = jnp.float32
BF16 = jnp.bfloat16
TOKEN_TILE = 256
PIECE = 256
V7X_VMEM_BYTES = 64 * 1024 * 1024
VMEM_LIMIT_BYTES = V7X_VMEM_BYTES - 1024 * 1024
HEAD_ORDER = (("ret", 0), ("hg", 0), ("hg", 1), ("ret", 1), ("hg", 2), ("hg", 3),
              ("ret", 2), ("hg", 4), ("hg", 5), ("ret", 3), ("hg", 6), ("hg", 7))
LN_ROWS = 16
LN_BLOCKS_PER_PIECE = 4


def _head_major(w):
    cols = []
    for h in range(RET_HEADS):
        cols += [w[:, OFF_RQ + h * RET_DK:OFF_RQ + (h + 1) * RET_DK], w[:, OFF_RK + h * RET_DK:OFF_RK + (h + 1) * RET_DK],
                 w[:, OFF_RV + h * RET_DV:OFF_RV + (h + 1) * RET_DV], w[:, OFF_RG + h * RET_DV:OFF_RG + (h + 1) * RET_DV]]
    for h in range(HG_HEADS):
        cols += [w[:, OFF_HQ + h * HG_DK:OFF_HQ + (h + 1) * HG_DK], w[:, OFF_HF + h * HG_DK:OFF_HF + (h + 1) * HG_DK],
                 w[:, OFF_HI + h * HG_DV:OFF_HI + (h + 1) * HG_DV], w[:, OFF_HG + h * HG_DV:OFF_HG + (h + 1) * HG_DV]]
    return jnp.concatenate(cols, axis=1)


def _sigmoid(x):
    return 1.0 / (1.0 + jnp.exp(-x))


def _ada_kernel(c_ref, w_ref, b_ref, o_ref):
    c = c_ref[...]
    cond = c * _sigmoid(c)
    o_ref[...] = jnp.dot(cond, w_ref[...], preferred_element_type=F32) + b_ref[...]


def _ada_call(c, w, b):
    bsz, d = c.shape
    n = w.shape[1]
    tn = 512 if n % 512 == 0 else n
    return pl.pallas_call(
        _ada_kernel,
        grid=(n // tn,),
        in_specs=[pl.BlockSpec((bsz, d), lambda j: (0, 0)),
                  pl.BlockSpec((d, tn), lambda j: (0, j)),
                  pl.BlockSpec((1, tn), lambda j: (0, j))],
        out_specs=pl.BlockSpec((bsz, tn), lambda j: (0, j)),
        out_shape=jax.ShapeDtypeStruct((bsz, n), F32),
        compiler_params=pltpu.CompilerParams(dimension_semantics=("arbitrary",)),
        name="ada_mod",
    )(c, w, b.reshape(1, n))


def _dot(a, b):
    return jnp.dot(a, b, preferred_element_type=F32)


def _dot_nt(a, b):
    return lax.dot_general(a, b, (((1,), (1,)), ((), ())), preferred_element_type=F32)


def _dot_tn(a, b):
    return lax.dot_general(a, b, (((0,), (0,)), ((), ())), preferred_element_type=F32)


def _split3(x):
    hi = x.astype(BF16)
    r1 = x - hi.astype(F32)
    mid = r1.astype(BF16)
    lo = (r1 - mid.astype(F32)).astype(BF16)
    return hi, mid, lo


def _layernorm(x):
    mu = jnp.mean(x, axis=-1, keepdims=True)
    xc = x - mu
    var = jnp.mean(xc * xc, axis=-1, keepdims=True)
    return xc * lax.rsqrt(var + EPS)


def _rmsnorm(x):
    return x * lax.rsqrt(jnp.mean(x * x, axis=-1, keepdims=True) + EPS)


def _fused_kernel(x_ref, xhbm_ref, pos_ref, shift_ref, scale_ref, gate_ref, win_ref, wout_ref, lnw_ref,
                  lnb_ref, lbl_ref, hgw_ref, dmask_ref, zeta_ref, xi_ref, tribd_ref, freq_ref, sign_ref,
                  o_ref, pret_ref, phg_ref, sret_ref, shg_ref, mixed_ref, xres_ref, h_ref, res_sem,
                  *, tm, steps_per_batch):
    s = pl.program_id(0)
    n_ret_chunks = tm // RET_CHUNK
    n_hg_chunks = tm // HG_CHUNK

    @pl.when(s == 0)
    def _():
        pret_ref[...] = jnp.zeros_like(pret_ref)
        phg_ref[...] = jnp.zeros_like(phg_ref)
        mixed_ref[...] = jnp.zeros_like(mixed_ref)

    @pl.when(lax.rem(jnp.maximum(s - 1, 0), steps_per_batch) == 0)
    def _():
        sret_ref[...] = jnp.zeros_like(sret_ref)
        shg_ref[...] = jnp.zeros_like(shg_ref)

    res_copy = pltpu.make_async_copy(xhbm_ref.at[pl.ds(jnp.maximum(s - 2, 0) * tm, tm), :], xres_ref, res_sem)
    res_copy.start()

    prev_q = []
    pending = []

    def fill(n=1):
        for _ in range(n):
            while prev_q or pending:
                is_matmul, thunk = (prev_q or pending).pop(0)
                thunk()
                if is_matmul:
                    break

    def finish_previous_tile():
        while prev_q:
            fill()

    def release(kind, h):
        pending.extend((True, p) for p in proj_pieces(kind, h))

    def proj_pieces(kind, h):
        dst_ref, gbase, gwidth = ((pret_ref, h * RET_GROUP, RET_GROUP) if kind == "ret"
                                  else (phg_ref, HG_BASE + h * HG_GROUP, HG_GROUP))

        def piece(c0):
            def run():
                dst_ref[h, :, pl.ds(c0, PIECE)] = _dot(h_ref[...], win_ref[:, pl.ds(gbase + c0, PIECE)])
            return run

        return [piece(c0) for c0 in range(0, gwidth, PIECE)]

    def outproj_piece(c0):
        def run():
            o_ref[:, pl.ds(c0, PIECE)] = _dot(mixed_ref[...], wout_ref[:, pl.ds(c0, PIECE)])
        return run

    def final_layernorm():
        res_copy.wait()
        for r0 in range(0, tm, LN_ROWS):
            rows = pl.ds(r0, LN_ROWS)
            z = ALPHA * xres_ref[rows, :] + gate_ref[0] * o_ref[rows, :]
            o_ref[rows, :] = _layernorm(z) * lnw_ref[...] + lnb_ref[...]

    prev_q.extend((True, outproj_piece(c0)) for c0 in range(0, o_ref.shape[1], PIECE))
    prev_q.append((False, final_layernorm))

    for i, r0 in enumerate(range(0, tm, LN_ROWS)):
        rows = pl.ds(r0, LN_ROWS)
        h_ref[rows, :] = (_layernorm(x_ref[rows, :]) * (1.0 + scale_ref[0]) + shift_ref[0]).astype(BF16)
        if i % LN_BLOCKS_PER_PIECE == LN_BLOCKS_PER_PIECE - 1:
            fill()

    ang = pos_ref[0].astype(F32) * freq_ref[...]
    cosf = jnp.cos(ang)
    sinf = jnp.sin(ang) * sign_ref[...]
    fill()

    def mix_ret(h):
        cdec = float(np.exp(RET_CHUNK * RET_LOG_DECAY[h]))
        scores, kv, qx, vs = [], [], [], []
        for c in range(n_ret_chunks):
            rows = pl.ds(c * RET_CHUNK, RET_CHUNK)
            cf = cosf[c * RET_CHUNK:(c + 1) * RET_CHUNK]
            sf = sinf[c * RET_CHUNK:(c + 1) * RET_CHUNK]
            q = pret_ref[h, rows, pl.ds(0, RET_DK)]
            k = pret_ref[h, rows, pl.ds(RET_DK, RET_DK)]
            v = pret_ref[h, rows, pl.ds(2 * RET_DK, RET_DV)].astype(BF16)
            q = (q * cf + pltpu.roll(q, RET_DK // 2, 1) * sf) * (RET_DK ** -0.5)
            k = k * cf + pltpu.roll(k, RET_DK // 2, 1) * sf
            scores.append(_dot_nt(q.astype(BF16), k.astype(BF16)))
            kv.append(_dot_tn((k * zeta_ref[h]).astype(BF16), v))
            qx.append((q * xi_ref[h]).astype(BF16))
            vs.append(v)
        fill()
        state = sret_ref[h]
        rets = []
        for c in range(n_ret_chunks):
            p = (scores[c] * dmask_ref[h]).astype(BF16)
            rets.append(_dot(p, vs[c]) + _dot(qx[c], state.astype(BF16)))
            state = cdec * state + kv[c]
        sret_ref[h] = state
        fill()
        finish_previous_tile()
        for c in range(n_ret_chunks):
            rows = pl.ds(c * RET_CHUNK, RET_CHUNK)
            g = pret_ref[h, rows, pl.ds(2 * RET_DK + RET_DV, RET_DV)]
            mixed_ref[rows, pl.ds(h * RET_DV, RET_DV)] = (_rmsnorm(rets[c]) * (g * _sigmoid(g))).astype(BF16)
        fill()
        release("ret", h)

    l0 = lbl_ref[0:1, :]
    l1 = lbl_ref[1:2, :]
    lmax = jnp.maximum(l0, l1)
    e0 = jnp.exp(l0 - lmax)
    lb_all = e0 / (e0 + jnp.exp(l1 - lmax))
    row = lax.broadcasted_iota(jnp.int32, (HG_CHUNK, HG_CHUNK), 0)
    col = lax.broadcasted_iota(jnp.int32, (HG_CHUNK, HG_CHUNK), 1)
    causal = row >= col

    def mix_hg(h):
        lb = lb_all[:, h * HG_DK:(h + 1) * HG_DK]
        fp = phg_ref[h, :, pl.ds(HG_DK, HG_DK)]
        sg = _sigmoid(fp)
        logf = jnp.log(lb + (1.0 - lb) * sg)
        kk = (1.0 - lb) * (1.0 - sg)
        parts = _dot(tribd_ref[...], jnp.concatenate(_split3(logf), axis=1))
        b = parts[:, 0:HG_DK] + parts[:, HG_DK:2 * HG_DK] + parts[:, 2 * HG_DK:3 * HG_DK]
        fill()
        hq = phg_ref[h, :, pl.ds(0, HG_DK)]
        qd = (hq * jnp.exp(b)).astype(BF16)
        kd = (kk * jnp.exp(-b)).astype(BF16)
        attn, kvt, elast, vs = [], [], [], []
        for c in range(n_hg_chunks):
            r0, r1 = c * HG_CHUNK, (c + 1) * HG_CHUNK
            b_last = b[r1 - 1:r1, :]
            ks = (kk[r0:r1] * jnp.exp(b_last - b[r0:r1])).astype(BF16)
            v = phg_ref[h, pl.ds(r0, HG_CHUNK), pl.ds(2 * HG_DK, HG_DV)].astype(BF16)
            attn.append(_dot_nt(qd[r0:r1], kd[r0:r1]))
            kvt.append(_dot_tn(v, ks))
            elast.append(jnp.exp(b_last))
            vs.append(v)
        fill()
        state_t = shg_ref[h]
        outs = []
        for c in range(n_hg_chunks):
            r0, r1 = c * HG_CHUNK, (c + 1) * HG_CHUNK
            a = jnp.where(causal, attn[c], 0.0).astype(BF16)
            outs.append(_dot(a, vs[c]) + _dot_nt(qd[r0:r1], state_t.astype(BF16)))
            state_t = state_t * elast[c] + kvt[c]
        shg_ref[h] = state_t
        fill()
        finish_previous_tile()
        hgw = hgw_ref[:, pl.ds(h * HG_DV, HG_DV)]
        for c in range(n_hg_chunks):
            rows = pl.ds(c * HG_CHUNK, HG_CHUNK)
            g = phg_ref[h, rows, pl.ds(2 * HG_DK + HG_DV, HG_DV)]
            mixed_ref[rows, pl.ds(RET_WIDTH + h * HG_DV, HG_DV)] = (
                _rmsnorm(outs[c]) * hgw * (g * _sigmoid(g))).astype(BF16)
        fill()
        release("hg", h)

    for kind, h in HEAD_ORDER:
        if kind == "ret":
            mix_ret(h)
        else:
            mix_hg(h)
    fill(len(pending))


def _ret_constants():
    lg = np.asarray(RET_LOG_DECAY, np.float64)
    idx = np.arange(RET_CHUNK, dtype=np.float64)
    diff = idx[:, None] - idx[None, :]
    dmask = np.where(diff >= 0, np.exp(np.where(diff >= 0, diff, 0.0)[None] * lg[:, None, None]), 0.0)
    zeta = np.exp((RET_CHUNK - 1 - idx)[None, :] * lg[:, None])
    xi = np.exp((idx + 1)[None, :] * lg[:, None])
    bc = lambda a: np.broadcast_to(a[:, :, None], (RET_HEADS, RET_CHUNK, RET_DK))
    return jnp.asarray(dmask, F32), jnp.asarray(bc(zeta), F32), jnp.asarray(bc(xi), F32)


def _chunk_tril(tm):
    i = np.arange(tm)
    m = (i[:, None] >= i[None, :]) & (i[:, None] // HG_CHUNK == i[None, :] // HG_CHUNK)
    return jnp.asarray(m, BF16)


def _fused_call(x2d, pos3, mod3, w_in_bf16, w_out_bf16, ln_w, ln_b, lb_logits, hg_norm_w, seq):
    m, d = x2d.shape
    tm = min(TOKEN_TILE, seq)
    steps_per_batch = seq // tm
    n_tiles = m // tm
    dmask, zeta, xi = _ret_constants()
    tribd = _chunk_tril(tm)
    half = RET_DK // 2
    freqs = ROPE_BASE ** (-jnp.arange(half, dtype=F32) / half)
    freq2 = jnp.concatenate([freqs, freqs]).reshape(1, RET_DK)
    sign = jnp.concatenate([-jnp.ones((half,), F32), jnp.ones((half,), F32)]).reshape(1, RET_DK)

    last = n_tiles - 1
    cur = lambda s: jnp.minimum(s, last)
    mixd = lambda s: jnp.clip(s - 1, 0, last)
    done = lambda s: jnp.maximum(s - 2, 0)
    resident = lambda a: pl.BlockSpec(a.shape, lambda s: (0,) * a.ndim, pipeline_mode=pl.Buffered(1))
    return pl.pallas_call(
        functools.partial(_fused_kernel, tm=tm, steps_per_batch=steps_per_batch),
        grid=(n_tiles + 2,),
        in_specs=[pl.BlockSpec((tm, d), lambda s: (cur(s), 0)),
                  pl.BlockSpec(memory_space=pl.ANY),
                  pl.BlockSpec((1, tm, 1), lambda s: (mixd(s) // steps_per_batch, mixd(s) % steps_per_batch, 0)),
                  pl.BlockSpec((1, 1, d), lambda s: (cur(s) // steps_per_batch, 0, 0)),
                  pl.BlockSpec((1, 1, d), lambda s: (cur(s) // steps_per_batch, 0, 1)),
                  pl.BlockSpec((1, 1, d), lambda s: (done(s) // steps_per_batch, 0, 2)),
                  resident(w_in_bf16), resident(w_out_bf16), resident(ln_w), resident(ln_b),
                  resident(lb_logits), resident(hg_norm_w), resident(dmask), resident(zeta),
                  resident(xi), resident(tribd), resident(freq2), resident(sign)],
        out_specs=pl.BlockSpec((tm, d), lambda s: (done(s), 0)),
        out_shape=jax.ShapeDtypeStruct((m, d), F32),
        scratch_shapes=[pltpu.VMEM((RET_HEADS, tm, RET_GROUP), F32),
                        pltpu.VMEM((HG_HEADS, tm, HG_GROUP), F32),
                        pltpu.VMEM((RET_HEADS, RET_DK, RET_DV), F32),
                        pltpu.VMEM((HG_HEADS, HG_DV, HG_DK), F32),
                        pltpu.VMEM((tm, MIX_WIDTH), BF16),
                        pltpu.VMEM((tm, d), F32),
                        pltpu.VMEM((tm, d), BF16),
                        pltpu.SemaphoreType.DMA(())],
        compiler_params=pltpu.CompilerParams(dimension_semantics=("arbitrary",),
                                             vmem_limit_bytes=VMEM_LIMIT_BYTES),
        name="fused_layer",
    )(x2d, x2d, pos3, mod3, mod3, mod3, w_in_bf16, w_out_bf16, ln_w, ln_b, lb_logits, hg_norm_w,
      dmask, zeta, xi, tribd, freq2, sign)


def kernel(x, c, positions, w_ada, b_ada, w_in, lb_logits, hg_norm_w, w_out, ln_w, ln_b):
    bsz, seq, d = x.shape
    assert w_ada.shape[0] == DEPTH and seq % RET_CHUNK == 0 and d % PIECE == 0
    mod = _ada_call(c, w_ada[0], b_ada[0])
    mod3 = mod.reshape(bsz, 1, 3 * d)
    out = _fused_call(x.reshape(bsz * seq, d), positions.reshape(bsz, seq, 1), mod3,
                      _head_major(w_in[0].astype(BF16)),
                      w_out[0].astype(BF16), ln_w[0].reshape(1, d), ln_b[0].reshape(1, d),
                      lb_logits.astype(F32), hg_norm_w[0].reshape(1, HG_WIDTH), seq)
    return out.reshape(bsz, seq, d)
```

```python
import functools

import numpy as np
import jax
import jax.numpy as jnp
from jax import lax
from jax.experimental import pallas as pl
from jax.experimental.pallas import tpu as pltpu

RET_HEADS = 4
RET_DK = 128
RET_DV = 256
HG_HEADS = 8
HG_DK = 128
HG_DV = 128
RET_CHUNK = 128
HG_CHUNK = 64
ROPE_BASE = 10000.0
EPS = 1e-6
DEPTH = 1
ALPHA = (2.0 * DEPTH) ** 0.25
RET_LOG2_DECAY = tuple(5.0 + 7.0 * h / (RET_HEADS - 1) for h in range(RET_HEADS))
RET_LOG_DECAY = tuple(float(np.log1p(-np.exp2(-e))) for e in RET_LOG2_DECAY)

RET_QK = RET_HEADS * RET_DK
RET_WIDTH = RET_HEADS * RET_DV
HG_QK = HG_HEADS * HG_DK
HG_WIDTH = HG_HEADS * HG_DV
MIX_WIDTH = RET_WIDTH + HG_WIDTH
OFF_RQ = 0
OFF_RK = OFF_RQ + RET_QK
OFF_RV = OFF_RK + RET_QK
OFF_RG = OFF_RV + RET_WIDTH
OFF_HQ = OFF_RG + RET_WIDTH
OFF_HF = OFF_HQ + HG_QK
OFF_HI = OFF_HF + HG_QK
OFF_HG = OFF_HI + HG_WIDTH
IN_WIDTH = OFF_HG + HG_WIDTH
RET_GROUP = 2 * RET_DK + 2 * RET_DV
HG_GROUP = 2 * HG_DK + 2 * HG_DV
HG_BASE = RET_HEADS * RET_GROUP

F32 = jnp.float32
BF16 = jnp.bfloat16
TOKEN_TILE = 256
PIECE = 256
V7X_VMEM_BYTES = 64 * 1024 * 1024
VMEM_LIMIT_BYTES = V7X_VMEM_BYTES - 1024 * 1024
HEAD_ORDER = (("ret", 0), ("hg", 0), ("hg", 1), ("ret", 1), ("hg", 2), ("hg", 3),
              ("ret", 2), ("hg", 4), ("hg", 5), ("ret", 3), ("hg", 6), ("hg", 7))
LN_ROWS = 16
LN_BLOCKS_PER_PIECE = 4


def _head_major(w):
    cols = []
    for h in range(RET_HEADS):
        cols += [w[:, OFF_RQ + h * RET_DK:OFF_RQ + (h + 1) * RET_DK], w[:, OFF_RK + h * RET_DK:OFF_RK + (h + 1) * RET_DK],
                 w[:, OFF_RV + h * RET_DV:OFF_RV + (h + 1) * RET_DV], w[:, OFF_RG + h * RET_DV:OFF_RG + (h + 1) * RET_DV]]
    for h in range(HG_HEADS):
        cols += [w[:, OFF_HQ + h * HG_DK:OFF_HQ + (h + 1) * HG_DK], w[:, OFF_HF + h * HG_DK:OFF_HF + (h + 1) * HG_DK],
                 w[:, OFF_HI + h * HG_DV:OFF_HI + (h + 1) * HG_DV], w[:, OFF_HG + h * HG_DV:OFF_HG + (h + 1) * HG_DV]]
    return jnp.concatenate(cols, axis=1)


def _sigmoid(x):
    return 1.0 / (1.0 + jnp.exp(-x))


def _ada_kernel(c_ref, w_ref, b_ref, o_ref):
    c = c_ref[...]
    cond = c * _sigmoid(c)
    o_ref[...] = jnp.dot(cond, w_ref[...], preferred_element_type=F32) + b_ref[...]


def _ada_call(c, w, b):
    bsz, d = c.shape
    n = w.shape[1]
    tn = 512 if n % 512 == 0 else n
    return pl.pallas_call(
        _ada_kernel,
        grid=(n // tn,),
        in_specs=[pl.BlockSpec((bsz, d), lambda j: (0, 0)),
                  pl.BlockSpec((d, tn), lambda j: (0, j)),
                  pl.BlockSpec((1, tn), lambda j: (0, j))],
        out_specs=pl.BlockSpec((bsz, tn), lambda j: (0, j)),
        out_shape=jax.ShapeDtypeStruct((bsz, n), F32),
        compiler_params=pltpu.CompilerParams(dimension_semantics=("arbitrary",)),
        name="ada_mod",
    )(c, w, b.reshape(1, n))


def _dot(a, b):
    return jnp.dot(a, b, preferred_element_type=F32)


def _dot_nt(a, b):
    return lax.dot_general(a, b, (((1,), (1,)), ((), ())), preferred_element_type=F32)


def _dot_tn(a, b):
    return lax.dot_general(a, b, (((0,), (0,)), ((), ())), preferred_element_type=F32)


def _split3(x):
    hi = x.astype(BF16)
    r1 = x - hi.astype(F32)
    mid = r1.astype(BF16)
    lo = (r1 - mid.astype(F32)).astype(BF16)
    return hi, mid, lo


def _layernorm(x):
    mu = jnp.mean(x, axis=-1, keepdims=True)
    xc = x - mu
    var = jnp.mean(xc * xc, axis=-1, keepdims=True)
    return xc * lax.rsqrt(var + EPS)


def _rmsnorm(x):
    return x * lax.rsqrt(jnp.mean(x * x, axis=-1, keepdims=True) + EPS)


def _fused_kernel(x_ref, xhbm_ref, pos_ref, shift_ref, scale_ref, gate_ref, win_ref, wout_ref, lnw_ref,
                  lnb_ref, lbl_ref, hgw_ref, dmask_ref, zeta_ref, xi_ref, tribd_ref, freq_ref, sign_ref,
                  o_ref, pret_ref, phg_ref, sret_ref, shg_ref, mixed_ref, xres_ref, h_ref, res_sem,
                  *, tm, steps_per_batch):
    s = pl.program_id(0)
    n_ret_chunks = tm // RET_CHUNK
    n_hg_chunks = tm // HG_CHUNK

    def residual_copy(step):
        tile = jnp.maximum(step - 2, 0)
        return pltpu.make_async_copy(xhbm_ref.at[pl.ds(tile * tm, tm), :], xres_ref, res_sem)

    @pl.when(s == 0)
    def _():
        pret_ref[...] = jnp.zeros_like(pret_ref)
        phg_ref[...] = jnp.zeros_like(phg_ref)
        mixed_ref[...] = jnp.zeros_like(mixed_ref)
        residual_copy(s).start()

    @pl.when(lax.rem(jnp.maximum(s - 1, 0), steps_per_batch) == 0)
    def _():
        sret_ref[...] = jnp.zeros_like(sret_ref)
        shg_ref[...] = jnp.zeros_like(shg_ref)

    prev_q = []
    pending = []

    def fill(n=1):
        for _ in range(n):
            while prev_q or pending:
                is_matmul, thunk = (prev_q or pending).pop(0)
                thunk()
                if is_matmul:
                    break

    def finish_previous_tile():
        while prev_q:
            fill()

    def release(kind, h):
        pending.extend((True, p) for p in proj_pieces(kind, h))

    def proj_pieces(kind, h):
        dst_ref, gbase, gwidth = ((pret_ref, h * RET_GROUP, RET_GROUP) if kind == "ret"
                                  else (phg_ref, HG_BASE + h * HG_GROUP, HG_GROUP))

        def piece(c0):
            def run():
                dst_ref[h, :, pl.ds(c0, PIECE)] = _dot(h_ref[...], win_ref[:, pl.ds(gbase + c0, PIECE)])
            return run

        return [piece(c0) for c0 in range(0, gwidth, PIECE)]

    def outproj_piece(c0):
        def run():
            o_ref[:, pl.ds(c0, PIECE)] = _dot(mixed_ref[...], wout_ref[:, pl.ds(c0, PIECE)])
        return run

    def final_layernorm():
        residual_copy(s).wait()
        for r0 in range(0, tm, LN_ROWS):
            rows = pl.ds(r0, LN_ROWS)
            z = ALPHA * xres_ref[rows, :] + gate_ref[0] * o_ref[rows, :]
            o_ref[rows, :] = _layernorm(z) * lnw_ref[...] + lnb_ref[...]

        @pl.when(s + 1 < pl.num_programs(0))
        def _():
            residual_copy(s + 1).start()

    prev_q.extend((True, outproj_piece(c0)) for c0 in range(0, o_ref.shape[1], PIECE))
    prev_q.append((False, final_layernorm))

    for i, r0 in enumerate(range(0, tm, LN_ROWS)):
        rows = pl.ds(r0, LN_ROWS)
        h_ref[rows, :] = (_layernorm(x_ref[rows, :]) * (1.0 + scale_ref[0]) + shift_ref[0]).astype(BF16)
        if i % LN_BLOCKS_PER_PIECE == LN_BLOCKS_PER_PIECE - 1:
            fill()

    ang = pos_ref[0].astype(F32) * freq_ref[...]
    cosf = jnp.cos(ang)
    sinf = jnp.sin(ang) * sign_ref[...]
    fill()

    def mix_ret(h):
        cdec = float(np.exp(RET_CHUNK * RET_LOG_DECAY[h]))
        scores, kv, qx, vs = [], [], [], []
        for c in range(n_ret_chunks):
            rows = pl.ds(c * RET_CHUNK, RET_CHUNK)
            cf = cosf[c * RET_CHUNK:(c + 1) * RET_CHUNK]
            sf = sinf[c * RET_CHUNK:(c + 1) * RET_CHUNK]
            q = pret_ref[h, rows, pl.ds(0, RET_DK)]
            k = pret_ref[h, rows, pl.ds(RET_DK, RET_DK)]
            v = pret_ref[h, rows, pl.ds(2 * RET_DK, RET_DV)].astype(BF16)
            q = (q * cf + pltpu.roll(q, RET_DK // 2, 1) * sf) * (RET_DK ** -0.5)
            k = k * cf + pltpu.roll(k, RET_DK // 2, 1) * sf
            scores.append(_dot_nt(q.astype(BF16), k.astype(BF16)))
            kv.append(_dot_tn((k * zeta_ref[h]).astype(BF16), v))
            qx.append((q * xi_ref[h]).astype(BF16))
            vs.append(v)
        fill()
        state = sret_ref[h]
        rets = []
        for c in range(n_ret_chunks):
            p = (scores[c] * dmask_ref[h]).astype(BF16)
            rets.append(_dot(p, vs[c]) + _dot(qx[c], state.astype(BF16)))
            state = cdec * state + kv[c]
        sret_ref[h] = state
        fill()
        finish_previous_tile()
        for c in range(n_ret_chunks):
            rows = pl.ds(c * RET_CHUNK, RET_CHUNK)
            g = pret_ref[h, rows, pl.ds(2 * RET_DK + RET_DV, RET_DV)]
            mixed_ref[rows, pl.ds(h * RET_DV, RET_DV)] = (_rmsnorm(rets[c]) * (g * _sigmoid(g))).astype(BF16)
        fill()
        release("ret", h)

    l0 = lbl_ref[0:1, :]
    l1 = lbl_ref[1:2, :]
    lmax = jnp.maximum(l0, l1)
    e0 = jnp.exp(l0 - lmax)
    lb_all = e0 / (e0 + jnp.exp(l1 - lmax))
    row = lax.broadcasted_iota(jnp.int32, (HG_CHUNK, HG_CHUNK), 0)
    col = lax.broadcasted_iota(jnp.int32, (HG_CHUNK, HG_CHUNK), 1)
    causal = row >= col

    def mix_hg(h):
        lb = lb_all[:, h * HG_DK:(h + 1) * HG_DK]
        fp = phg_ref[h, :, pl.ds(HG_DK, HG_DK)]
        sg = _sigmoid(fp)
        logf = jnp.log(lb + (1.0 - lb) * sg)
        kk = (1.0 - lb) * (1.0 - sg)
        parts = _dot(tribd_ref[...], jnp.concatenate(_split3(logf), axis=1))
        b = parts[:, 0:HG_DK] + parts[:, HG_DK:2 * HG_DK] + parts[:, 2 * HG_DK:3 * HG_DK]
        fill()
        hq = phg_ref[h, :, pl.ds(0, HG_DK)]
        qd = (hq * jnp.exp(b)).astype(BF16)
        kd = (kk * jnp.exp(-b)).astype(BF16)
        attn, kvt, elast, vs = [], [], [], []
        for c in range(n_hg_chunks):
            r0, r1 = c * HG_CHUNK, (c + 1) * HG_CHUNK
            b_last = b[r1 - 1:r1, :]
            ks = (kk[r0:r1] * jnp.exp(b_last - b[r0:r1])).astype(BF16)
            v = phg_ref[h, pl.ds(r0, HG_CHUNK), pl.ds(2 * HG_DK, HG_DV)].astype(BF16)
            attn.append(_dot_nt(qd[r0:r1], kd[r0:r1]))
            kvt.append(_dot_tn(v, ks))
            elast.append(jnp.exp(b_last))
            vs.append(v)
        fill()
        state_t = shg_ref[h]
        outs = []
        for c in range(n_hg_chunks):
            r0, r1 = c * HG_CHUNK, (c + 1) * HG_CHUNK
            a = jnp.where(causal, attn[c], 0.0).astype(BF16)
            outs.append(_dot(a, vs[c]) + _dot_nt(qd[r0:r1], state_t.astype(BF16)))
            state_t = state_t * elast[c] + kvt[c]
        shg_ref[h] = state_t
        fill()
        finish_previous_tile()
        hgw = hgw_ref[:, pl.ds(h * HG_DV, HG_DV)]
        for c in range(n_hg_chunks):
            rows = pl.ds(c * HG_CHUNK, HG_CHUNK)
            g = phg_ref[h, rows, pl.ds(2 * HG_DK + HG_DV, HG_DV)]
            mixed_ref[rows, pl.ds(RET_WIDTH + h * HG_DV, HG_DV)] = (
                _rmsnorm(outs[c]) * hgw * (g * _sigmoid(g))).astype(BF16)
        fill()
        release("hg", h)

    for kind, h in HEAD_ORDER:
        if kind == "ret":
            mix_ret(h)
        else:
            mix_hg(h)
    fill(len(pending))


def _ret_constants():
    lg = np.asarray(RET_LOG_DECAY, np.float64)
    idx = np.arange(RET_CHUNK, dtype=np.float64)
    diff = idx[:, None] - idx[None, :]
    dmask = np.where(diff >= 0, np.exp(np.where(diff >= 0, diff, 0.0)[None] * lg[:, None, None]), 0.0)
    zeta = np.exp((RET_CHUNK - 1 - idx)[None, :] * lg[:, None])
    xi = np.exp((idx + 1)[None, :] * lg[:, None])
    bc = lambda a: np.broadcast_to(a[:, :, None], (RET_HEADS, RET_CHUNK, RET_DK))
    return jnp.asarray(dmask, F32), jnp.asarray(bc(zeta), F32), jnp.asarray(bc(xi), F32)


def _chunk_tril(tm):
    i = np.arange(tm)
    m = (i[:, None] >= i[None, :]) & (i[:, None] // HG_CHUNK == i[None, :] // HG_CHUNK)
    return jnp.asarray(m, BF16)


def _fused_call(x2d, pos3, mod3, w_in_bf16, w_out_bf16, ln_w, ln_b, lb_logits, hg_norm_w, seq):
    m, d = x2d.shape
    tm = min(TOKEN_TILE, seq)
    steps_per_batch = seq // tm
    n_tiles = m // tm
    dmask, zeta, xi = _ret_constants()
    tribd = _chunk_tril(tm)
    half = RET_DK // 2
    freqs = ROPE_BASE ** (-jnp.arange(half, dtype=F32) / half)
    freq2 = jnp.concatenate([freqs, freqs]).reshape(1, RET_DK)
    sign = jnp.concatenate([-jnp.ones((half,), F32), jnp.ones((half,), F32)]).reshape(1, RET_DK)

    last = n_tiles - 1
    cur = lambda s: jnp.minimum(s, last)
    mixd = lambda s: jnp.clip(s - 1, 0, last)
    done = lambda s: jnp.maximum(s - 2, 0)
    resident = lambda a: pl.BlockSpec(a.shape, lambda s: (0,) * a.ndim, pipeline_mode=pl.Buffered(1))
    return pl.pallas_call(
        functools.partial(_fused_kernel, tm=tm, steps_per_batch=steps_per_batch),
        grid=(n_tiles + 2,),
        in_specs=[pl.BlockSpec((tm, d), lambda s: (cur(s), 0)),
                  pl.BlockSpec(memory_space=pl.ANY),
                  pl.BlockSpec((1, tm, 1), lambda s: (mixd(s) // steps_per_batch, mixd(s) % steps_per_batch, 0)),
                  pl.BlockSpec((1, 1, d), lambda s: (cur(s) // steps_per_batch, 0, 0)),
                  pl.BlockSpec((1, 1, d), lambda s: (cur(s) // steps_per_batch, 0, 1)),
                  pl.BlockSpec((1, 1, d), lambda s: (done(s) // steps_per_batch, 0, 2)),
                  resident(w_in_bf16), resident(w_out_bf16), resident(ln_w), resident(ln_b),
                  resident(lb_logits), resident(hg_norm_w), resident(dmask), resident(zeta),
                  resident(xi), resident(tribd), resident(freq2), resident(sign)],
        out_specs=pl.BlockSpec((tm, d), lambda s: (done(s), 0)),
        out_shape=jax.ShapeDtypeStruct((m, d), F32),
        scratch_shapes=[pltpu.VMEM((RET_HEADS, tm, RET_GROUP), F32),
                        pltpu.VMEM((HG_HEADS, tm, HG_GROUP), F32),
                        pltpu.VMEM((RET_HEADS, RET_DK, RET_DV), F32),
                        pltpu.VMEM((HG_HEADS, HG_DV, HG_DK), F32),
                        pltpu.VMEM((tm, MIX_WIDTH), BF16),
                        pltpu.VMEM((tm, d), F32),
                        pltpu.VMEM((tm, d), BF16),
                        pltpu.SemaphoreType.DMA(())],
        compiler_params=pltpu.CompilerParams(dimension_semantics=("arbitrary",),
                                             vmem_limit_bytes=VMEM_LIMIT_BYTES),
        name="fused_layer",
    )(x2d, x2d, pos3, mod3, mod3, mod3, w_in_bf16, w_out_bf16, ln_w, ln_b, lb_logits, hg_norm_w,
      dmask, zeta, xi, tribd, freq2, sign)


def kernel(x, c, positions, w_ada, b_ada, w_in, lb_logits, hg_norm_w, w_out, ln_w, ln_b):
    bsz, seq, d = x.shape
    assert w_ada.shape[0] == DEPTH and seq % RET_CHUNK == 0 and d % PIECE == 0
    mod = _ada_call(c, w_ada[0], b_ada[0])
    mod3 = mod.reshape(bsz, 1, 3 * d)
    out = _fused_call(x.reshape(bsz * seq, d), positions.reshape(bsz, seq, 1), mod3,
                      _head_major(w_in[0].astype(BF16)),
                      w_out[0].astype(BF16), ln_w[0].reshape(1, d), ln_b[0].reshape(1, d),
                      lb_logits.astype(F32), hg_norm_w[0].reshape(1, HG_WIDTH), seq)
    return out.reshape(bsz, seq, d)
```

```python
import functools

import numpy as np
import jax
import jax.numpy as jnp
from jax import lax
from jax.experimental import pallas as pl
from jax.experimental.pallas import tpu as pltpu

RET_HEADS = 4
RET_DK = 128
RET_DV = 256
HG_HEADS = 8
HG_DK = 128
HG_DV = 128
RET_CHUNK = 128
HG_CHUNK = 64
ROPE_BASE = 10000.0
EPS = 1e-6
DEPTH = 1
ALPHA = (2.0 * DEPTH) ** 0.25
RET_LOG2_DECAY = tuple(5.0 + 7.0 * h / (RET_HEADS - 1) for h in range(RET_HEADS))
RET_LOG_DECAY = tuple(float(np.log1p(-np.exp2(-e))) for e in RET_LOG2_DECAY)

RET_QK = RET_HEADS * RET_DK
RET_WIDTH = RET_HEADS * RET_DV
HG_QK = HG_HEADS * HG_DK
HG_WIDTH = HG_HEADS * HG_DV
MIX_WIDTH = RET_WIDTH + HG_WIDTH
OFF_RQ = 0
OFF_RK = OFF_RQ + RET_QK
OFF_RV = OFF_RK + RET_QK
OFF_RG = OFF_RV + RET_WIDTH
OFF_HQ = OFF_RG + RET_WIDTH
OFF_HF = OFF_HQ + HG_QK
OFF_HI = OFF_HF + HG_QK
OFF_HG = OFF_HI + HG_WIDTH
IN_WIDTH = OFF_HG + HG_WIDTH
RET_GROUP = 2 * RET_DK + 2 * RET_DV
HG_GROUP = 2 * HG_DK + 2 * HG_DV

F32 = jnp.float32
BF16 = jnp.bfloat16
TOKEN_TILE = 256
PIECE = 256
V7X_VMEM_BYTES = 64 * 1024 * 1024
VMEM_LIMIT_BYTES = V7X_VMEM_BYTES - 1024 * 1024
HEAD_ORDER = (("ret", 0), ("ret", 1), ("hg", 0), ("hg", 1), ("hg", 2), ("hg", 3),
              ("ret", 2), ("ret", 3), ("hg", 4), ("hg", 5), ("hg", 6), ("hg", 7))
LN_ROWS = 16
LN_BLOCKS_PER_PIECE = 4


def _column_slabs(w):
    k, n = w.shape
    return w.reshape(k, n // PIECE, PIECE).transpose(1, 0, 2)


def _sigmoid(x):
    return 1.0 / (1.0 + jnp.exp(-x))


def _ada_kernel(c_ref, w_ref, b_ref, o_ref):
    c = c_ref[...]
    cond = c * _sigmoid(c)
    o_ref[...] = jnp.dot(cond, w_ref[...], preferred_element_type=F32) + b_ref[...]


def _ada_call(c, w, b):
    bsz, d = c.shape
    n = w.shape[1]
    tn = 512 if n % 512 == 0 else n
    return pl.pallas_call(
        _ada_kernel,
        grid=(n // tn,),
        in_specs=[pl.BlockSpec((bsz, d), lambda j: (0, 0)),
                  pl.BlockSpec((d, tn), lambda j: (0, j)),
                  pl.BlockSpec((1, tn), lambda j: (0, j))],
        out_specs=pl.BlockSpec((bsz, tn), lambda j: (0, j)),
        out_shape=jax.ShapeDtypeStruct((bsz, n), F32),
        compiler_params=pltpu.CompilerParams(dimension_semantics=("arbitrary",)),
        name="ada_mod",
    )(c, w, b.reshape(1, n))


def _dot(a, b):
    return jnp.dot(a, b, preferred_element_type=F32)


def _dot_nt(a, b):
    return lax.dot_general(a, b, (((1,), (1,)), ((), ())), preferred_element_type=F32)


def _dot_tn(a, b):
    return lax.dot_general(a, b, (((0,), (0,)), ((), ())), preferred_element_type=F32)


def _split3(x):
    hi = x.astype(BF16)
    r1 = x - hi.astype(F32)
    mid = r1.astype(BF16)
    lo = (r1 - mid.astype(F32)).astype(BF16)
    return hi, mid, lo


def _layernorm(x):
    mu = jnp.mean(x, axis=-1, keepdims=True)
    xc = x - mu
    var = jnp.mean(xc * xc, axis=-1, keepdims=True)
    return xc * lax.rsqrt(var + EPS)


def _rmsnorm(x):
    return x * lax.rsqrt(jnp.mean(x * x, axis=-1, keepdims=True) + EPS)


def _fused_kernel(x_ref, xhbm_ref, pos_ref, shift_ref, scale_ref, gate_ref, win_ref, wout_ref, lnw_ref,
                  lnb_ref, lbl_ref, hgw_ref, dmask_ref, zeta_ref, xi_ref, tribd_ref, freq_ref, sign_ref,
                  o_ref, pret_ref, phg_ref, sret_ref, shg_ref, mixed_ref, xres_ref, h_ref, res_sem,
                  *, tm, steps_per_batch):
    s = pl.program_id(0)
    n_ret_chunks = tm // RET_CHUNK
    n_hg_chunks = tm // HG_CHUNK

    def residual_copy(step):
        tile = jnp.maximum(step - 2, 0)
        return pltpu.make_async_copy(xhbm_ref.at[pl.ds(tile * tm, tm), :], xres_ref, res_sem)

    @pl.when(s == 0)
    def _():
        pret_ref[...] = jnp.zeros_like(pret_ref)
        phg_ref[...] = jnp.zeros_like(phg_ref)
        mixed_ref[...] = jnp.zeros_like(mixed_ref)
        residual_copy(s).start()

    @pl.when(lax.rem(jnp.maximum(s - 1, 0), steps_per_batch) == 0)
    def _():
        sret_ref[...] = jnp.zeros_like(sret_ref)
        shg_ref[...] = jnp.zeros_like(shg_ref)

    residual_copy(s).wait()

    prev_q = []
    pending = []

    def fill(n=1):
        for _ in range(n):
            while prev_q or pending:
                is_matmul, thunk = (prev_q or pending).pop(0)
                thunk()
                if is_matmul:
                    break

    def finish_previous_tile():
        while prev_q:
            fill()

    mixed_heads = set()

    def release(kind, h):
        mixed_heads.add((kind, h))
        if (kind, h ^ 1) in mixed_heads:
            pending.extend((True, p) for p in proj_pieces(kind, h & ~1))

    def proj_pieces(kind, h0):
        dst_ref = pret_ref if kind == "ret" else phg_ref

        def piece(src_col, dests):
            def run():
                r = _dot(h_ref[...], win_ref[src_col // PIECE])
                c = 0
                for head, col, width in dests:
                    dst_ref[head, :, pl.ds(col, width)] = r[:, c:c + width]
                    c += width
            return run

        if kind == "ret":
            pieces = [piece(off + h0 * RET_DK, [(h0, col, RET_DK), (h0 + 1, col, RET_DK)])
                      for off, col in ((OFF_RQ, 0), (OFF_RK, RET_DK))]
            pieces += [piece(off + h * RET_DV, [(h, col, RET_DV)])
                       for off, col in ((OFF_RV, 2 * RET_DK), (OFF_RG, 2 * RET_DK + RET_DV))
                       for h in (h0, h0 + 1)]
            return pieces
        return [piece(off + h0 * HG_DK, [(h0, col, HG_DK), (h0 + 1, col, HG_DK)])
                for off, col in ((OFF_HQ, 0), (OFF_HF, HG_DK), (OFF_HI, 2 * HG_DK), (OFF_HG, 2 * HG_DK + HG_DV))]

    def outproj_piece(c0):
        def run():
            o_ref[:, pl.ds(c0, PIECE)] = _dot(mixed_ref[...], wout_ref[c0 // PIECE])
        return run

    def final_layernorm():
        for r0 in range(0, tm, LN_ROWS):
            rows = pl.ds(r0, LN_ROWS)
            z = ALPHA * xres_ref[rows, :] + gate_ref[0] * o_ref[rows, :]
            o_ref[rows, :] = _layernorm(z) * lnw_ref[...] + lnb_ref[...]

        @pl.when(s + 1 < pl.num_programs(0))
        def _():
            residual_copy(s + 1).start()

    prev_q.extend((True, outproj_piece(c0)) for c0 in range(0, o_ref.shape[1], PIECE))
    prev_q.append((False, final_layernorm))

    for i, r0 in enumerate(range(0, tm, LN_ROWS)):
        rows = pl.ds(r0, LN_ROWS)
        h_ref[rows, :] = (_layernorm(x_ref[rows, :]) * (1.0 + scale_ref[0]) + shift_ref[0]).astype(BF16)
        if i % LN_BLOCKS_PER_PIECE == LN_BLOCKS_PER_PIECE - 1:
            fill()

    ang = pos_ref[0].astype(F32) * freq_ref[...]
    cosf = jnp.cos(ang)
    sinf = jnp.sin(ang) * sign_ref[...]
    fill()

    def mix_ret(h):
        cdec = float(np.exp(RET_CHUNK * RET_LOG_DECAY[h]))
        scores, kv, qx, vs = [], [], [], []
        for c in range(n_ret_chunks):
            rows = pl.ds(c * RET_CHUNK, RET_CHUNK)
            cf = cosf[c * RET_CHUNK:(c + 1) * RET_CHUNK]
            sf = sinf[c * RET_CHUNK:(c + 1) * RET_CHUNK]
            q = pret_ref[h, rows, pl.ds(0, RET_DK)]
            k = pret_ref[h, rows, pl.ds(RET_DK, RET_DK)]
            v = pret_ref[h, rows, pl.ds(2 * RET_DK, RET_DV)].astype(BF16)
            q = (q * cf + pltpu.roll(q, RET_DK // 2, 1) * sf) * (RET_DK ** -0.5)
            k = k * cf + pltpu.roll(k, RET_DK // 2, 1) * sf
            scores.append(_dot_nt(q.astype(BF16), k.astype(BF16)))
            kv.append(_dot_tn((k * zeta_ref[h]).astype(BF16), v))
            qx.append((q * xi_ref[h]).astype(BF16))
            vs.append(v)
        fill()
        state = sret_ref[h]
        rets = []
        for c in range(n_ret_chunks):
            p = (scores[c] * dmask_ref[h]).astype(BF16)
            rets.append(_dot(p, vs[c]) + _dot(qx[c], state.astype(BF16)))
            state = cdec * state + kv[c]
        sret_ref[h] = state
        fill()
        finish_previous_tile()
        for c in range(n_ret_chunks):
            rows = pl.ds(c * RET_CHUNK, RET_CHUNK)
            g = pret_ref[h, rows, pl.ds(2 * RET_DK + RET_DV, RET_DV)]
            mixed_ref[rows, pl.ds(h * RET_DV, RET_DV)] = (_rmsnorm(rets[c]) * (g * _sigmoid(g))).astype(BF16)
        fill()
        release("ret", h)

    l0 = lbl_ref[0:1, :]
    l1 = lbl_ref[1:2, :]
    lmax = jnp.maximum(l0, l1)
    e0 = jnp.exp(l0 - lmax)
    lb_all = e0 / (e0 + jnp.exp(l1 - lmax))
    row = lax.broadcasted_iota(jnp.int32, (HG_CHUNK, HG_CHUNK), 0)
    col = lax.broadcasted_iota(jnp.int32, (HG_CHUNK, HG_CHUNK), 1)
    causal = row >= col

    def mix_hg(h):
        lb = lb_all[:, h * HG_DK:(h + 1) * HG_DK]
        fp = phg_ref[h, :, pl.ds(HG_DK, HG_DK)]
        sg = _sigmoid(fp)
        logf = jnp.log(lb + (1.0 - lb) * sg)
        kk = (1.0 - lb) * (1.0 - sg)
        parts = _dot(tribd_ref[...], jnp.concatenate(_split3(logf), axis=1))
        b = parts[:, 0:HG_DK] + parts[:, HG_DK:2 * HG_DK] + parts[:, 2 * HG_DK:3 * HG_DK]
        fill()
        hq = phg_ref[h, :, pl.ds(0, HG_DK)]
        qd = (hq * jnp.exp(b)).astype(BF16)
        kd = (kk * jnp.exp(-b)).astype(BF16)
        attn, kvt, elast, vs = [], [], [], []
        for c in range(n_hg_chunks):
            r0, r1 = c * HG_CHUNK, (c + 1) * HG_CHUNK
            b_last = b[r1 - 1:r1, :]
            ks = (kk[r0:r1] * jnp.exp(b_last - b[r0:r1])).astype(BF16)
            v = phg_ref[h, pl.ds(r0, HG_CHUNK), pl.ds(2 * HG_DK, HG_DV)].astype(BF16)
            attn.append(_dot_nt(qd[r0:r1], kd[r0:r1]))
            kvt.append(_dot_tn(v, ks))
            elast.append(jnp.exp(b_last))
            vs.append(v)
        fill()
        state_t = shg_ref[h]
        outs = []
        for c in range(n_hg_chunks):
            r0, r1 = c * HG_CHUNK, (c + 1) * HG_CHUNK
            a = jnp.where(causal, attn[c], 0.0).astype(BF16)
            outs.append(_dot(a, vs[c]) + _dot_nt(qd[r0:r1], state_t.astype(BF16)))
            state_t = state_t * elast[c] + kvt[c]
        shg_ref[h] = state_t
        fill()
        finish_previous_tile()
        hgw = hgw_ref[:, pl.ds(h * HG_DV, HG_DV)]
        for c in range(n_hg_chunks):
            rows = pl.ds(c * HG_CHUNK, HG_CHUNK)
            g = phg_ref[h, rows, pl.ds(2 * HG_DK + HG_DV, HG_DV)]
            mixed_ref[rows, pl.ds(RET_WIDTH + h * HG_DV, HG_DV)] = (
                _rmsnorm(outs[c]) * hgw * (g * _sigmoid(g))).astype(BF16)
        fill()
        release("hg", h)

    for kind, h in HEAD_ORDER:
        if kind == "ret":
            mix_ret(h)
        else:
            mix_hg(h)
    fill(len(pending))


def _ret_constants():
    lg = np.asarray(RET_LOG_DECAY, np.float64)
    idx = np.arange(RET_CHUNK, dtype=np.float64)
    diff = idx[:, None] - idx[None, :]
    dmask = np.where(diff >= 0, np.exp(np.where(diff >= 0, diff, 0.0)[None] * lg[:, None, None]), 0.0)
    zeta = np.exp((RET_CHUNK - 1 - idx)[None, :] * lg[:, None])
    xi = np.exp((idx + 1)[None, :] * lg[:, None])
    bc = lambda a: np.broadcast_to(a[:, :, None], (RET_HEADS, RET_CHUNK, RET_DK))
    return jnp.asarray(dmask, F32), jnp.asarray(bc(zeta), F32), jnp.asarray(bc(xi), F32)


def _chunk_tril(tm):
    i = np.arange(tm)
    m = (i[:, None] >= i[None, :]) & (i[:, None] // HG_CHUNK == i[None, :] // HG_CHUNK)
    return jnp.asarray(m, BF16)


def _fused_call(x2d, pos3, mod3, w_in_slabs, w_out_slabs, ln_w, ln_b, lb_logits, hg_norm_w, seq):
    m, d = x2d.shape
    tm = min(TOKEN_TILE, seq)
    steps_per_batch = seq // tm
    n_tiles = m // tm
    dmask, zeta, xi = _ret_constants()
    tribd = _chunk_tril(tm)
    half = RET_DK // 2
    freqs = ROPE_BASE ** (-jnp.arange(half, dtype=F32) / half)
    freq2 = jnp.concatenate([freqs, freqs]).reshape(1, RET_DK)
    sign = jnp.concatenate([-jnp.ones((half,), F32), jnp.ones((half,), F32)]).reshape(1, RET_DK)

    last = n_tiles - 1
    cur = lambda s: jnp.minimum(s, last)
    mixd = lambda s: jnp.clip(s - 1, 0, last)
    done = lambda s: jnp.maximum(s - 2, 0)
    resident = lambda a: pl.BlockSpec(a.shape, lambda s: (0,) * a.ndim, pipeline_mode=pl.Buffered(1))
    return pl.pallas_call(
        functools.partial(_fused_kernel, tm=tm, steps_per_batch=steps_per_batch),
        grid=(n_tiles + 2,),
        in_specs=[pl.BlockSpec((tm, d), lambda s: (cur(s), 0)),
                  pl.BlockSpec(memory_space=pl.ANY),
                  pl.BlockSpec((1, tm, 1), lambda s: (mixd(s) // steps_per_batch, mixd(s) % steps_per_batch, 0)),
                  pl.BlockSpec((1, 1, d), lambda s: (cur(s) // steps_per_batch, 0, 0)),
                  pl.BlockSpec((1, 1, d), lambda s: (cur(s) // steps_per_batch, 0, 1)),
                  pl.BlockSpec((1, 1, d), lambda s: (done(s) // steps_per_batch, 0, 2)),
                  resident(w_in_slabs), resident(w_out_slabs), resident(ln_w), resident(ln_b),
                  resident(lb_logits), resident(hg_norm_w), resident(dmask), resident(zeta),
                  resident(xi), resident(tribd), resident(freq2), resident(sign)],
        out_specs=pl.BlockSpec((tm, d), lambda s: (done(s), 0), pipeline_mode=pl.Buffered(1)),
        out_shape=jax.ShapeDtypeStruct((m, d), F32),
        scratch_shapes=[pltpu.VMEM((RET_HEADS, tm, RET_GROUP), F32),
                        pltpu.VMEM((HG_HEADS, tm, HG_GROUP), F32),
                        pltpu.VMEM((RET_HEADS, RET_DK, RET_DV), F32),
                        pltpu.VMEM((HG_HEADS, HG_DV, HG_DK), F32),
                        pltpu.VMEM((tm, MIX_WIDTH), BF16),
                        pltpu.VMEM((tm, d), F32),
                        pltpu.VMEM((tm, d), BF16),
                        pltpu.SemaphoreType.DMA(())],
        compiler_params=pltpu.CompilerParams(dimension_semantics=("arbitrary",),
                                             vmem_limit_bytes=VMEM_LIMIT_BYTES),
        name="fused_layer",
    )(x2d, x2d, pos3, mod3, mod3, mod3, w_in_slabs, w_out_slabs, ln_w, ln_b, lb_logits, hg_norm_w,
      dmask, zeta, xi, tribd, freq2, sign)


def kernel(x, c, positions, w_ada, b_ada, w_in, lb_logits, hg_norm_w, w_out, ln_w, ln_b):
    bsz, seq, d = x.shape
    assert w_ada.shape[0] == DEPTH and seq % RET_CHUNK == 0 and d % PIECE == 0
    mod = _ada_call(c, w_ada[0], b_ada[0])
    mod3 = mod.reshape(bsz, 1, 3 * d)
    out = _fused_call(x.reshape(bsz * seq, d), positions.reshape(bsz, seq, 1), mod3,
                      _column_slabs(w_in[0].astype(BF16)), _column_slabs(w_out[0].astype(BF16)),
                      ln_w[0].reshape(1, d), ln_b[0].reshape(1, d),
                      lb_logits.astype(F32), hg_norm_w[0].reshape(1, HG_WIDTH), seq)
    return out.reshape(bsz, seq, d)
```

```python
import functools

import numpy as np
import jax
import jax.numpy as jnp
from jax import lax
from jax.experimental import pallas as pl
from jax.experimental.pallas import tpu as pltpu

RET_HEADS = 4
RET_DK = 128
RET_DV = 256
HG_HEADS = 8
HG_DK = 128
HG_DV = 128
RET_CHUNK = 128
HG_CHUNK = 64
ROPE_BASE = 10000.0
EPS = 1e-6
DEPTH = 1
ALPHA = (2.0 * DEPTH) ** 0.25
RET_LOG2_DECAY = tuple(5.0 + 7.0 * h / (RET_HEADS - 1) for h in range(RET_HEADS))
RET_LOG_DECAY = tuple(float(np.log1p(-np.exp2(-e))) for e in RET_LOG2_DECAY)

RET_QK = RET_HEADS * RET_DK
RET_WIDTH = RET_HEADS * RET_DV
HG_QK = HG_HEADS * HG_DK
HG_WIDTH = HG_HEADS * HG_DV
MIX_WIDTH = RET_WIDTH + HG_WIDTH
OFF_RQ = 0
OFF_RK = OFF_RQ + RET_QK
OFF_RV = OFF_RK + RET_QK
OFF_RG = OFF_RV + RET_WIDTH
OFF_HQ = OFF_RG + RET_WIDTH
OFF_HF = OFF_HQ + HG_QK
OFF_HI = OFF_HF + HG_QK
OFF_HG = OFF_HI + HG_WIDTH
IN_WIDTH = OFF_HG + HG_WIDTH
RET_GROUP = 2 * RET_DK + 2 * RET_DV
HG_GROUP = 2 * HG_DK + 2 * HG_DV

F32 = jnp.float32
BF16 = jnp.bfloat16
TOKEN_TILE = 256
PIECE = 256
V7X_VMEM_BYTES = 64 * 1024 * 1024
VMEM_LIMIT_BYTES = V7X_VMEM_BYTES - 512 * 1024
HEAD_ORDER = (("ret", 0), ("ret", 1), ("hg", 0), ("hg", 1), ("hg", 2), ("hg", 3),
              ("ret", 2), ("ret", 3), ("hg", 4), ("hg", 5), ("hg", 6), ("hg", 7))
MAX_FAST_EXPONENT = 80.0
LN_ROWS = 16
LN_BLOCKS_PER_PIECE = 4


def _column_slabs(w):
    k, n = w.shape
    return w.reshape(k, n // PIECE, PIECE).transpose(1, 0, 2)


def _sigmoid(x):
    return 1.0 / (1.0 + jnp.exp(-x))


def _ada_kernel(c_ref, w_ref, b_ref, o_ref):
    c = c_ref[...]
    cond = c * _sigmoid(c)
    o_ref[...] = jnp.dot(cond, w_ref[...], preferred_element_type=F32) + b_ref[...]


def _ada_call(c, w, b):
    bsz, d = c.shape
    n = w.shape[1]
    tn = 512 if n % 512 == 0 else n
    return pl.pallas_call(
        _ada_kernel,
        grid=(n // tn,),
        in_specs=[pl.BlockSpec((bsz, d), lambda j: (0, 0)),
                  pl.BlockSpec((d, tn), lambda j: (0, j)),
                  pl.BlockSpec((1, tn), lambda j: (0, j))],
        out_specs=pl.BlockSpec((bsz, tn), lambda j: (0, j)),
        out_shape=jax.ShapeDtypeStruct((bsz, n), F32),
        compiler_params=pltpu.CompilerParams(dimension_semantics=("arbitrary",)),
        name="ada_mod",
    )(c, w, b.reshape(1, n))


def _dot(a, b):
    return jnp.dot(a, b, preferred_element_type=F32)


def _dot_nt(a, b):
    return lax.dot_general(a, b, (((1,), (1,)), ((), ())), preferred_element_type=F32)


def _dot_tn(a, b):
    return lax.dot_general(a, b, (((0,), (0,)), ((), ())), preferred_element_type=F32)


def _split3(x):
    hi = x.astype(BF16)
    r1 = x - hi.astype(F32)
    mid = r1.astype(BF16)
    lo = (r1 - mid.astype(F32)).astype(BF16)
    return hi, mid, lo


def _layernorm(x):
    mu = jnp.mean(x, axis=-1, keepdims=True)
    xc = x - mu
    var = jnp.mean(xc * xc, axis=-1, keepdims=True)
    return xc * lax.rsqrt(var + EPS)


def _rmsnorm(x):
    return x * lax.rsqrt(jnp.mean(x * x, axis=-1, keepdims=True) + EPS)


def _fused_kernel(xhbm_ref, pos_ref, shift_ref, scale_ref, gate_ref, win_ref, wout_ref, lnw_ref,
                  lnb_ref, lbl_ref, hgw_ref, dmask_ref, zeta_ref, xi_ref, tribd_ref, freq_ref, sign_ref,
                  o_ref, pret_ref, phg_ref, sret_ref, shg_ref, mixed_ref, xin_ref, xres_ref, h_ref,
                  in_sem, res_sem, *, tm, steps_per_batch, n_tiles, safe):
    s = pl.program_id(0)
    n_ret_chunks = tm // RET_CHUNK
    n_hg_chunks = tm // HG_CHUNK

    def residual_copy(step):
        tile = jnp.maximum(step - 2, 0)
        return pltpu.make_async_copy(xhbm_ref.at[pl.ds(tile * tm, tm), :], xres_ref, res_sem)

    def input_copy(step):
        tile = jnp.minimum(step, n_tiles - 1)
        return pltpu.make_async_copy(xhbm_ref.at[pl.ds(tile * tm, tm), :], xin_ref, in_sem)

    @pl.when(s == 0)
    def _():
        pret_ref[...] = jnp.zeros_like(pret_ref)
        phg_ref[...] = jnp.zeros_like(phg_ref)
        mixed_ref[...] = jnp.zeros_like(mixed_ref)
        residual_copy(s).start()
        input_copy(s).start()

    @pl.when(lax.rem(jnp.maximum(s - 1, 0), steps_per_batch) == 0)
    def _():
        sret_ref[...] = jnp.zeros_like(sret_ref)
        shg_ref[...] = jnp.zeros_like(shg_ref)

    residual_copy(s).wait()
    input_copy(s).wait()

    prev_q = []
    pending = []

    def fill(n=1):
        for _ in range(n):
            while prev_q or pending:
                is_matmul, thunk = (prev_q or pending).pop(0)
                thunk()
                if is_matmul:
                    break

    def finish_previous_tile():
        while prev_q:
            fill()

    mixed_heads = set()

    def release(kind, h):
        mixed_heads.add((kind, h))
        if (kind, h ^ 1) in mixed_heads:
            pending.extend((True, p) for p in proj_pieces(kind, h & ~1))

    def proj_pieces(kind, h0):
        dst_ref = pret_ref if kind == "ret" else phg_ref

        def piece(src_col, dests):
            def run():
                r = _dot(h_ref[...], win_ref[src_col // PIECE])
                c = 0
                for head, col, width in dests:
                    dst_ref[head, :, pl.ds(col, width)] = r[:, c:c + width]
                    c += width
            return run

        if kind == "ret":
            pieces = [piece(off + h0 * RET_DK, [(h0, col, RET_DK), (h0 + 1, col, RET_DK)])
                      for off, col in ((OFF_RQ, 0), (OFF_RK, RET_DK))]
            pieces += [piece(off + h * RET_DV, [(h, col, RET_DV)])
                       for off, col in ((OFF_RV, 2 * RET_DK), (OFF_RG, 2 * RET_DK + RET_DV))
                       for h in (h0, h0 + 1)]
            return pieces
        return [piece(off + h0 * HG_DK, [(h0, col, HG_DK), (h0 + 1, col, HG_DK)])
                for off, col in ((OFF_HQ, 0), (OFF_HF, HG_DK), (OFF_HI, 2 * HG_DK), (OFF_HG, 2 * HG_DK + HG_DV))]

    def outproj_piece(c0):
        def run():
            o_ref[:, pl.ds(c0, PIECE)] = _dot(mixed_ref[...], wout_ref[c0 // PIECE])
        return run

    def final_layernorm():
        for r0 in range(0, tm, LN_ROWS):
            rows = pl.ds(r0, LN_ROWS)
            z = ALPHA * xres_ref[rows, :] + gate_ref[0] * o_ref[rows, :]
            o_ref[rows, :] = _layernorm(z) * lnw_ref[...] + lnb_ref[...]

        @pl.when(s + 1 < pl.num_programs(0))
        def _():
            residual_copy(s + 1).start()

    prev_q.extend((True, outproj_piece(c0)) for c0 in range(0, o_ref.shape[1], PIECE))
    prev_q.append((False, final_layernorm))

    for i, r0 in enumerate(range(0, tm, LN_ROWS)):
        rows = pl.ds(r0, LN_ROWS)
        h_ref[rows, :] = (_layernorm(xin_ref[rows, :]) * (1.0 + scale_ref[0]) + shift_ref[0]).astype(BF16)
        if i % LN_BLOCKS_PER_PIECE == LN_BLOCKS_PER_PIECE - 1:
            fill()

    @pl.when(s + 1 < pl.num_programs(0))
    def _():
        input_copy(s + 1).start()

    ang = pos_ref[0].astype(F32) * freq_ref[...]
    cosf = jnp.cos(ang)
    sinf = jnp.sin(ang) * sign_ref[...]
    fill()

    def mix_ret(h):
        cdec = float(np.exp(RET_CHUNK * RET_LOG_DECAY[h]))
        scores, kv, qx, vs = [], [], [], []
        for c in range(n_ret_chunks):
            rows = pl.ds(c * RET_CHUNK, RET_CHUNK)
            cf = cosf[c * RET_CHUNK:(c + 1) * RET_CHUNK]
            sf = sinf[c * RET_CHUNK:(c + 1) * RET_CHUNK]
            q = pret_ref[h, rows, pl.ds(0, RET_DK)]
            k = pret_ref[h, rows, pl.ds(RET_DK, RET_DK)]
            v = pret_ref[h, rows, pl.ds(2 * RET_DK, RET_DV)].astype(BF16)
            q = (q * cf + pltpu.roll(q, RET_DK // 2, 1) * sf) * (RET_DK ** -0.5)
            k = k * cf + pltpu.roll(k, RET_DK // 2, 1) * sf
            scores.append(_dot_nt(q.astype(BF16), k.astype(BF16)))
            kv.append(_dot_tn((k * zeta_ref[h]).astype(BF16), v))
            qx.append((q * xi_ref[h]).astype(BF16))
            vs.append(v)
        fill()
        state = sret_ref[h]
        rets = []
        for c in range(n_ret_chunks):
            p = (scores[c] * dmask_ref[h]).astype(BF16)
            rets.append(_dot(p, vs[c]) + _dot(qx[c], state.astype(BF16)))
            state = cdec * state + kv[c]
        sret_ref[h] = state
        fill()
        finish_previous_tile()
        for c in range(n_ret_chunks):
            rows = pl.ds(c * RET_CHUNK, RET_CHUNK)
            g = pret_ref[h, rows, pl.ds(2 * RET_DK + RET_DV, RET_DV)]
            mixed_ref[rows, pl.ds(h * RET_DV, RET_DV)] = (_rmsnorm(rets[c]) * (g * _sigmoid(g))).astype(BF16)
        fill()
        release("ret", h)

    l0 = lbl_ref[0:1, :]
    l1 = lbl_ref[1:2, :]
    lmax = jnp.maximum(l0, l1)
    e0 = jnp.exp(l0 - lmax)
    lb_all = e0 / (e0 + jnp.exp(l1 - lmax))
    row = lax.broadcasted_iota(jnp.int32, (HG_CHUNK, HG_CHUNK), 0)
    col = lax.broadcasted_iota(jnp.int32, (HG_CHUNK, HG_CHUNK), 1)
    causal = row >= col

    def intra_chunk_scores_safe(q, k, bc):
        trow = lax.broadcasted_iota(jnp.int32, (HG_CHUNK, HG_DK), 0)
        xor = jnp.bitwise_xor(row, col)
        acc = jnp.where(row == col, jnp.sum(q * k, axis=-1, keepdims=True), 0.0)
        half = 1
        while half < HG_CHUNK:
            block = 2 * half
            upper = jnp.bitwise_and(trow, half) != 0
            if block >= 8:
                b3 = bc.reshape(HG_CHUNK // block, block, HG_DK)
                bm = jnp.broadcast_to(b3[:, half - 1:half, :], b3.shape).reshape(HG_CHUNK, HG_DK)
            elif half == 2:
                b3 = bc.reshape(HG_CHUNK // 8, 8, HG_DK)
                sub = lax.broadcasted_iota(jnp.int32, b3.shape, 1)
                bm = jnp.where(sub < 4, jnp.broadcast_to(b3[:, 1:2, :], b3.shape),
                               jnp.broadcast_to(b3[:, 5:6, :], b3.shape)).reshape(HG_CHUNK, HG_DK)
            else:
                bm = jnp.where(upper, pltpu.roll(bc, 1, 0), bc)
            u = (jnp.where(upper, q, k) * jnp.exp(-jnp.abs(bc - bm))).astype(BF16)
            level = (row > col) & (xor >= half) & (xor < block)
            acc = jnp.where(level, _dot_nt(u, u), acc)
            half = block
        return acc

    def mix_hg(h):
        lb = lb_all[:, h * HG_DK:(h + 1) * HG_DK]
        fp = phg_ref[h, :, pl.ds(HG_DK, HG_DK)]
        sg = _sigmoid(fp)
        logf = jnp.log(lb + (1.0 - lb) * sg)
        kk = (1.0 - lb) * (1.0 - sg)
        parts = _dot(tribd_ref[...], jnp.concatenate(_split3(logf), axis=1))
        b = parts[:, 0:HG_DK] + parts[:, HG_DK:2 * HG_DK] + parts[:, 2 * HG_DK:3 * HG_DK]
        fill()
        hq = phg_ref[h, :, pl.ds(0, HG_DK)]
        qd = (hq * jnp.exp(b)).astype(BF16)
        if not safe:
            kd = (kk * jnp.exp(-b)).astype(BF16)
        attn, kvt, elast, vs = [], [], [], []
        for c in range(n_hg_chunks):
            r0, r1 = c * HG_CHUNK, (c + 1) * HG_CHUNK
            b_last = b[r1 - 1:r1, :]
            ks = (kk[r0:r1] * jnp.exp(b_last - b[r0:r1])).astype(BF16)
            v = phg_ref[h, pl.ds(r0, HG_CHUNK), pl.ds(2 * HG_DK, HG_DV)].astype(BF16)
            if safe:
                attn.append(intra_chunk_scores_safe(hq[r0:r1], kk[r0:r1], b[r0:r1]))
            else:
                attn.append(jnp.where(causal, _dot_nt(qd[r0:r1], kd[r0:r1]), 0.0))
            kvt.append(_dot_tn(v, ks))
            elast.append(jnp.exp(b_last))
            vs.append(v)
        fill()
        state_t = shg_ref[h]
        outs = []
        for c in range(n_hg_chunks):
            r0, r1 = c * HG_CHUNK, (c + 1) * HG_CHUNK
            outs.append(_dot(attn[c].astype(BF16), vs[c]) + _dot_nt(qd[r0:r1], state_t.astype(BF16)))
            state_t = state_t * elast[c] + kvt[c]
        shg_ref[h] = state_t
        fill()
        finish_previous_tile()
        hgw = hgw_ref[:, pl.ds(h * HG_DV, HG_DV)]
        for c in range(n_hg_chunks):
            rows = pl.ds(c * HG_CHUNK, HG_CHUNK)
            g = phg_ref[h, rows, pl.ds(2 * HG_DK + HG_DV, HG_DV)]
            mixed_ref[rows, pl.ds(RET_WIDTH + h * HG_DV, HG_DV)] = (
                _rmsnorm(outs[c]) * hgw * (g * _sigmoid(g))).astype(BF16)
        fill()
        release("hg", h)

    for kind, h in HEAD_ORDER:
        if kind == "ret":
            mix_ret(h)
        else:
            mix_hg(h)
    fill(len(pending))


def _guarded_kernel(safe_ref, *refs, **static):
    @pl.when(safe_ref[0] == 0)
    def _():
        _fused_kernel(*refs, safe=False, **static)

    @pl.when(safe_ref[0] != 0)
    def _():
        _fused_kernel(*refs, safe=True, **static)


def _needs_safe_path(lb_logits):
    lg = lb_logits.astype(F32)
    neg_log_lb = -jax.nn.log_softmax(lg, axis=0)[0]
    return (HG_CHUNK * jnp.max(neg_log_lb) > MAX_FAST_EXPONENT).astype(jnp.int32).reshape(1)


def _ret_constants():
    lg = np.asarray(RET_LOG_DECAY, np.float64)
    idx = np.arange(RET_CHUNK, dtype=np.float64)
    diff = idx[:, None] - idx[None, :]
    dmask = np.where(diff >= 0, np.exp(np.where(diff >= 0, diff, 0.0)[None] * lg[:, None, None]), 0.0)
    zeta = np.exp((RET_CHUNK - 1 - idx)[None, :] * lg[:, None])
    xi = np.exp((idx + 1)[None, :] * lg[:, None])
    bc = lambda a: np.broadcast_to(a[:, :, None], (RET_HEADS, RET_CHUNK, RET_DK))
    return jnp.asarray(dmask, F32), jnp.asarray(bc(zeta), F32), jnp.asarray(bc(xi), F32)


def _chunk_tril(tm):
    i = np.arange(tm)
    m = (i[:, None] >= i[None, :]) & (i[:, None] // HG_CHUNK == i[None, :] // HG_CHUNK)
    return jnp.asarray(m, BF16)


def _fused_call(x2d, pos3, mod3, w_in_slabs, w_out_slabs, ln_w, ln_b, lb_logits, hg_norm_w, seq):
    m, d = x2d.shape
    tm = min(TOKEN_TILE, seq)
    steps_per_batch = seq // tm
    n_tiles = m // tm
    dmask, zeta, xi = _ret_constants()
    tribd = _chunk_tril(tm)
    half = RET_DK // 2
    freqs = ROPE_BASE ** (-jnp.arange(half, dtype=F32) / half)
    freq2 = jnp.concatenate([freqs, freqs]).reshape(1, RET_DK)
    sign = jnp.concatenate([-jnp.ones((half,), F32), jnp.ones((half,), F32)]).reshape(1, RET_DK)

    last = n_tiles - 1
    cur = lambda s: jnp.minimum(s, last)
    mixd = lambda s: jnp.clip(s - 1, 0, last)
    done = lambda s: jnp.maximum(s - 2, 0)
    resident = lambda a: pl.BlockSpec(a.shape, lambda s: (0,) * a.ndim, pipeline_mode=pl.Buffered(1))
    return pl.pallas_call(
        functools.partial(_guarded_kernel, tm=tm, steps_per_batch=steps_per_batch, n_tiles=n_tiles),
        grid=(n_tiles + 2,),
        in_specs=[pl.BlockSpec(memory_space=pltpu.SMEM),
                  pl.BlockSpec(memory_space=pl.ANY),
                  pl.BlockSpec((1, tm, 1), lambda s: (mixd(s) // steps_per_batch, mixd(s) % steps_per_batch, 0)),
                  pl.BlockSpec((1, 1, d), lambda s: (cur(s) // steps_per_batch, 0, 0)),
                  pl.BlockSpec((1, 1, d), lambda s: (cur(s) // steps_per_batch, 0, 1)),
                  pl.BlockSpec((1, 1, d), lambda s: (done(s) // steps_per_batch, 0, 2)),
                  resident(w_in_slabs), resident(w_out_slabs), resident(ln_w), resident(ln_b),
                  resident(lb_logits), resident(hg_norm_w), resident(dmask), resident(zeta),
                  resident(xi), resident(tribd), resident(freq2), resident(sign)],
        out_specs=pl.BlockSpec((tm, d), lambda s: (done(s), 0), pipeline_mode=pl.Buffered(1)),
        out_shape=jax.ShapeDtypeStruct((m, d), F32),
        scratch_shapes=[pltpu.VMEM((RET_HEADS, tm, RET_GROUP), F32),
                        pltpu.VMEM((HG_HEADS, tm, HG_GROUP), F32),
                        pltpu.VMEM((RET_HEADS, RET_DK, RET_DV), F32),
                        pltpu.VMEM((HG_HEADS, HG_DV, HG_DK), F32),
                        pltpu.VMEM((tm, MIX_WIDTH), BF16),
                        pltpu.VMEM((tm, d), F32),
                        pltpu.VMEM((tm, d), F32),
                        pltpu.VMEM((tm, d), BF16),
                        pltpu.SemaphoreType.DMA(()),
                        pltpu.SemaphoreType.DMA(())],
        compiler_params=pltpu.CompilerParams(dimension_semantics=("arbitrary",),
                                             vmem_limit_bytes=VMEM_LIMIT_BYTES),
        name="fused_layer",
    )(_needs_safe_path(lb_logits), x2d, pos3, mod3, mod3, mod3, w_in_slabs, w_out_slabs, ln_w, ln_b,
      lb_logits, hg_norm_w, dmask, zeta, xi, tribd, freq2, sign)


def kernel(x, c, positions, w_ada, b_ada, w_in, lb_logits, hg_norm_w, w_out, ln_w, ln_b):
    bsz, seq, d = x.shape
    assert w_ada.shape[0] == DEPTH and seq % RET_CHUNK == 0 and d % PIECE == 0
    mod = _ada_call(c, w_ada[0], b_ada[0])
    mod3 = mod.reshape(bsz, 1, 3 * d)
    out = _fused_call(x.reshape(bsz * seq, d), positions.reshape(bsz, seq, 1), mod3,
                      _column_slabs(w_in[0].astype(BF16)), _column_slabs(w_out[0].astype(BF16)),
                      ln_w[0].reshape(1, d), ln_b[0].reshape(1, d),
                      lb_logits.astype(F32), hg_norm_w[0].reshape(1, HG_WIDTH), seq)
    return out.reshape(bsz, seq, d)
```

```python
import functools

import numpy as np
import jax
import jax.numpy as jnp
from jax import lax
from jax.experimental import pallas as pl
from jax.experimental.pallas import tpu as pltpu

RET_HEADS = 4
RET_DK = 128
RET_DV = 256
HG_HEADS = 8
HG_DK = 128
HG_DV = 128
RET_CHUNK = 128
HG_CHUNK = 64
ROPE_BASE = 10000.0
EPS = 1e-6
DEPTH = 1
ALPHA = (2.0 * DEPTH) ** 0.25
RET_LOG2_DECAY = tuple(5.0 + 7.0 * h / (RET_HEADS - 1) for h in range(RET_HEADS))
RET_LOG_DECAY = tuple(float(np.log1p(-np.exp2(-e))) for e in RET_LOG2_DECAY)

RET_QK = RET_HEADS * RET_DK
RET_WIDTH = RET_HEADS * RET_DV
HG_QK = HG_HEADS * HG_DK
HG_WIDTH = HG_HEADS * HG_DV
MIX_WIDTH = RET_WIDTH + HG_WIDTH
OFF_RQ = 0
OFF_RK = OFF_RQ + RET_QK
OFF_RV = OFF_RK + RET_QK
OFF_RG = OFF_RV + RET_WIDTH
OFF_HQ = OFF_RG + RET_WIDTH
OFF_HF = OFF_HQ + HG_QK
OFF_HI = OFF_HF + HG_QK
OFF_HG = OFF_HI + HG_WIDTH
IN_WIDTH = OFF_HG + HG_WIDTH
RET_GROUP = 2 * RET_DK + 2 * RET_DV
HG_GROUP = 2 * HG_DK + 2 * HG_DV

F32 = jnp.float32
BF16 = jnp.bfloat16
TOKEN_TILE = 256
PIECE = 256
V7X_VMEM_BYTES = 64 * 1024 * 1024
VMEM_LIMIT_BYTES = V7X_VMEM_BYTES - 512 * 1024
HEAD_ORDER = (("ret", 0), ("ret", 1), ("hg", 0), ("hg", 1), ("hg", 2), ("hg", 3),
              ("ret", 2), ("ret", 3), ("hg", 4), ("hg", 5), ("hg", 6), ("hg", 7))
MAX_FAST_EXPONENT = 80.0
LN_ROWS = 16
LN_BLOCKS_PER_PIECE = 4


def _column_slabs(w):
    k, n = w.shape
    return w.reshape(k, n // PIECE, PIECE).transpose(1, 0, 2)


def _sigmoid(x):
    return 1.0 / (1.0 + jnp.exp(-x))


def _ada_kernel(c_ref, w_ref, b_ref, o_ref):
    c = c_ref[...]
    cond = c * _sigmoid(c)
    o_ref[...] = jnp.dot(cond, w_ref[...], preferred_element_type=F32) + b_ref[...]


def _ada_call(c, w, b):
    bsz, d = c.shape
    n = w.shape[1]
    tn = 512 if n % 512 == 0 else n
    return pl.pallas_call(
        _ada_kernel,
        grid=(n // tn,),
        in_specs=[pl.BlockSpec((bsz, d), lambda j: (0, 0)),
                  pl.BlockSpec((d, tn), lambda j: (0, j)),
                  pl.BlockSpec((1, tn), lambda j: (0, j))],
        out_specs=pl.BlockSpec((bsz, tn), lambda j: (0, j)),
        out_shape=jax.ShapeDtypeStruct((bsz, n), F32),
        compiler_params=pltpu.CompilerParams(dimension_semantics=("arbitrary",)),
        name="ada_mod",
    )(c, w, b.reshape(1, n))


def _dot(a, b):
    return jnp.dot(a, b, preferred_element_type=F32)


def _dot_nt(a, b):
    return lax.dot_general(a, b, (((1,), (1,)), ((), ())), preferred_element_type=F32)


def _dot_tn(a, b):
    return lax.dot_general(a, b, (((0,), (0,)), ((), ())), preferred_element_type=F32)


def _split3(x):
    hi = x.astype(BF16)
    r1 = x - hi.astype(F32)
    mid = r1.astype(BF16)
    lo = (r1 - mid.astype(F32)).astype(BF16)
    return hi, mid, lo


def _layernorm(x):
    mu = jnp.mean(x, axis=-1, keepdims=True)
    xc = x - mu
    var = jnp.mean(xc * xc, axis=-1, keepdims=True)
    return xc * lax.rsqrt(var + EPS)


def _rmsnorm(x):
    return x * lax.rsqrt(jnp.mean(x * x, axis=-1, keepdims=True) + EPS)


def _fused_kernel(xhbm_ref, pos_ref, shift_ref, scale_ref, gate_ref, win_ref, wout_ref, lnw_ref,
                  lnb_ref, lbl_ref, hgw_ref, dmask_ref, zeta_ref, xi_ref, tribd_ref, freq_ref, sign_ref,
                  o_ref, pret_ref, phg_ref, sret_ref, shg_ref, mixed_ref, xin_ref, xres_ref, h_ref,
                  in_sem, res_sem, *, tm, steps_per_batch, n_tiles, safe):
    s = pl.program_id(0)
    n_ret_chunks = tm // RET_CHUNK
    n_hg_chunks = tm // HG_CHUNK

    def residual_copy(step):
        tile = jnp.maximum(step - 2, 0)
        return pltpu.make_async_copy(xhbm_ref.at[pl.ds(tile * tm, tm), :], xres_ref, res_sem)

    def input_copy(step):
        tile = jnp.minimum(step, n_tiles - 1)
        return pltpu.make_async_copy(xhbm_ref.at[pl.ds(tile * tm, tm), :], xin_ref, in_sem)

    @pl.when(s == 0)
    def _():
        pret_ref[...] = jnp.zeros_like(pret_ref)
        phg_ref[...] = jnp.zeros_like(phg_ref)
        mixed_ref[...] = jnp.zeros_like(mixed_ref)
        residual_copy(s).start()
        input_copy(s).start()

    @pl.when(lax.rem(jnp.maximum(s - 1, 0), steps_per_batch) == 0)
    def _():
        sret_ref[...] = jnp.zeros_like(sret_ref)
        shg_ref[...] = jnp.zeros_like(shg_ref)

    residual_copy(s).wait()
    input_copy(s).wait()

    prev_q = []
    pending = []

    def fill(n=1):
        for _ in range(n):
            while prev_q or pending:
                is_matmul, thunk = (prev_q or pending).pop(0)
                thunk()
                if is_matmul:
                    break

    def finish_previous_tile():
        while prev_q:
            fill()

    mixed_heads = set()

    def release(kind, h):
        mixed_heads.add((kind, h))
        if (kind, h ^ 1) in mixed_heads:
            pending.extend((True, p) for p in proj_pieces(kind, h & ~1))

    def proj_pieces(kind, h0):
        dst_ref = pret_ref if kind == "ret" else phg_ref

        def piece(src_col, dests):
            def run():
                r = _dot(h_ref[...], win_ref[src_col // PIECE])
                c = 0
                for head, col, width in dests:
                    dst_ref[head, :, pl.ds(col, width)] = r[:, c:c + width]
                    c += width
            return run

        if kind == "ret":
            pieces = [piece(off + h0 * RET_DK, [(h0, col, RET_DK), (h0 + 1, col, RET_DK)])
                      for off, col in ((OFF_RQ, 0), (OFF_RK, RET_DK))]
            pieces += [piece(off + h * RET_DV, [(h, col, RET_DV)])
                       for off, col in ((OFF_RV, 2 * RET_DK), (OFF_RG, 2 * RET_DK + RET_DV))
                       for h in (h0, h0 + 1)]
            return pieces
        return [piece(off + h0 * HG_DK, [(h0, col, HG_DK), (h0 + 1, col, HG_DK)])
                for off, col in ((OFF_HQ, 0), (OFF_HF, HG_DK), (OFF_HI, 2 * HG_DK), (OFF_HG, 2 * HG_DK + HG_DV))]

    def outproj_piece(c0):
        def run():
            o_ref[:, pl.ds(c0, PIECE)] = _dot(mixed_ref[...], wout_ref[c0 // PIECE])
        return run

    def final_layernorm():
        for r0 in range(0, tm, LN_ROWS):
            rows = pl.ds(r0, LN_ROWS)
            z = ALPHA * xres_ref[rows, :] + gate_ref[0] * o_ref[rows, :]
            o_ref[rows, :] = _layernorm(z) * lnw_ref[...] + lnb_ref[...]

        @pl.when(s + 1 < pl.num_programs(0))
        def _():
            residual_copy(s + 1).start()

    prev_q.extend((True, outproj_piece(c0)) for c0 in range(0, o_ref.shape[1], PIECE))
    prev_q.append((False, final_layernorm))

    for i, r0 in enumerate(range(0, tm, LN_ROWS)):
        rows = pl.ds(r0, LN_ROWS)
        h_ref[rows, :] = (_layernorm(xin_ref[rows, :]) * (1.0 + scale_ref[0]) + shift_ref[0]).astype(BF16)
        if i % LN_BLOCKS_PER_PIECE == LN_BLOCKS_PER_PIECE - 1:
            fill()

    @pl.when(s + 1 < pl.num_programs(0))
    def _():
        input_copy(s + 1).start()

    ang = pos_ref[0].astype(F32) * freq_ref[...]
    cosf = jnp.cos(ang)
    sinf = jnp.sin(ang) * sign_ref[...]
    fill()

    def mix_ret(h):
        cdec = float(np.exp(RET_CHUNK * RET_LOG_DECAY[h]))
        scores, kv, qx, vs = [], [], [], []
        for c in range(n_ret_chunks):
            rows = pl.ds(c * RET_CHUNK, RET_CHUNK)
            cf = cosf[c * RET_CHUNK:(c + 1) * RET_CHUNK]
            sf = sinf[c * RET_CHUNK:(c + 1) * RET_CHUNK]
            q = pret_ref[h, rows, pl.ds(0, RET_DK)]
            k = pret_ref[h, rows, pl.ds(RET_DK, RET_DK)]
            v = pret_ref[h, rows, pl.ds(2 * RET_DK, RET_DV)].astype(BF16)
            q = (q * cf + pltpu.roll(q, RET_DK // 2, 1) * sf) * (RET_DK ** -0.5)
            k = k * cf + pltpu.roll(k, RET_DK // 2, 1) * sf
            scores.append(_dot_nt(q.astype(BF16), k.astype(BF16)))
            kv.append(_dot_tn((k * zeta_ref[h]).astype(BF16), v))
            qx.append((q * xi_ref[h]).astype(BF16))
            vs.append(v)
        fill()
        state = sret_ref[h]
        rets = []
        for c in range(n_ret_chunks):
            p = (scores[c] * dmask_ref[h]).astype(BF16)
            rets.append(_dot(p, vs[c]) + _dot(qx[c], state.astype(BF16)))
            state = cdec * state + kv[c]
        sret_ref[h] = state
        fill()
        finish_previous_tile()
        for c in range(n_ret_chunks):
            rows = pl.ds(c * RET_CHUNK, RET_CHUNK)
            g = pret_ref[h, rows, pl.ds(2 * RET_DK + RET_DV, RET_DV)]
            mixed_ref[rows, pl.ds(h * RET_DV, RET_DV)] = (_rmsnorm(rets[c]) * (g * _sigmoid(g))).astype(BF16)
        fill()
        release("ret", h)

    l0 = lbl_ref[0:1, :]
    l1 = lbl_ref[1:2, :]
    lmax = jnp.maximum(l0, l1)
    e0 = jnp.exp(l0 - lmax)
    lb_all = e0 / (e0 + jnp.exp(l1 - lmax))
    row = lax.broadcasted_iota(jnp.int32, (HG_CHUNK, HG_CHUNK), 0)
    col = lax.broadcasted_iota(jnp.int32, (HG_CHUNK, HG_CHUNK), 1)
    causal = row >= col

    def intra_chunk_scores_safe(q, k, bc):
        trow = lax.broadcasted_iota(jnp.int32, (HG_CHUNK, HG_DK), 0)
        xor = jnp.bitwise_xor(row, col)
        acc = jnp.where(row == col, jnp.sum(q * k, axis=-1, keepdims=True), 0.0)
        half = 1
        while half < HG_CHUNK:
            block = 2 * half
            upper = jnp.bitwise_and(trow, half) != 0
            if block >= 8:
                b3 = bc.reshape(HG_CHUNK // block, block, HG_DK)
                bm = jnp.broadcast_to(b3[:, half - 1:half, :], b3.shape).reshape(HG_CHUNK, HG_DK)
            elif half == 2:
                b3 = bc.reshape(HG_CHUNK // 8, 8, HG_DK)
                sub = lax.broadcasted_iota(jnp.int32, b3.shape, 1)
                bm = jnp.where(sub < 4, jnp.broadcast_to(b3[:, 1:2, :], b3.shape),
                               jnp.broadcast_to(b3[:, 5:6, :], b3.shape)).reshape(HG_CHUNK, HG_DK)
            else:
                bm = jnp.where(upper, pltpu.roll(bc, 1, 0), bc)
            u = (jnp.where(upper, q, k) * jnp.exp(-jnp.abs(bc - bm))).astype(BF16)
            level = (row > col) & (xor >= half) & (xor < block)
            acc = jnp.where(level, _dot_nt(u, u), acc)
            half = block
        return acc

    def mix_hg(h):
        lb = lb_all[:, h * HG_DK:(h + 1) * HG_DK]
        fp = phg_ref[h, :, pl.ds(HG_DK, HG_DK)]
        sg = _sigmoid(fp)
        logf = jnp.log(lb + (1.0 - lb) * sg)
        kk = (1.0 - lb) * (1.0 - sg)
        parts = _dot(tribd_ref[...], jnp.concatenate(_split3(logf), axis=1))
        b = parts[:, 0:HG_DK] + parts[:, HG_DK:2 * HG_DK] + parts[:, 2 * HG_DK:3 * HG_DK]
        fill()
        hq = phg_ref[h, :, pl.ds(0, HG_DK)]
        qd = (hq * jnp.exp(b)).astype(BF16)
        if not safe:
            kd = (kk * jnp.exp(-b)).astype(BF16)
        attn, kvt, elast, vs = [], [], [], []
        for c in range(n_hg_chunks):
            r0, r1 = c * HG_CHUNK, (c + 1) * HG_CHUNK
            b_last = b[r1 - 1:r1, :]
            ks = (kk[r0:r1] * jnp.exp(b_last - b[r0:r1])).astype(BF16)
            v = phg_ref[h, pl.ds(r0, HG_CHUNK), pl.ds(2 * HG_DK, HG_DV)].astype(BF16)
            if safe:
                attn.append(intra_chunk_scores_safe(hq[r0:r1], kk[r0:r1], b[r0:r1]))
            else:
                attn.append(jnp.where(causal, _dot_nt(qd[r0:r1], kd[r0:r1]), 0.0))
            kvt.append(_dot_tn(v, ks))
            elast.append(jnp.exp(b_last))
            vs.append(v)
        fill()
        state_t = shg_ref[h]
        outs = []
        for c in range(n_hg_chunks):
            r0, r1 = c * HG_CHUNK, (c + 1) * HG_CHUNK
            outs.append(_dot(attn[c].astype(BF16), vs[c]) + _dot_nt(qd[r0:r1], state_t.astype(BF16)))
            state_t = state_t * elast[c] + kvt[c]
        shg_ref[h] = state_t
        fill()
        finish_previous_tile()
        hgw = hgw_ref[:, pl.ds(h * HG_DV, HG_DV)]
        for c in range(n_hg_chunks):
            rows = pl.ds(c * HG_CHUNK, HG_CHUNK)
            g = phg_ref[h, rows, pl.ds(2 * HG_DK + HG_DV, HG_DV)]
            mixed_ref[rows, pl.ds(RET_WIDTH + h * HG_DV, HG_DV)] = (
                _rmsnorm(outs[c]) * hgw * (g * _sigmoid(g))).astype(BF16)
        fill()
        release("hg", h)

    for kind, h in HEAD_ORDER:
        if kind == "ret":
            mix_ret(h)
        else:
            mix_hg(h)
    fill(len(pending))


def _needs_safe_path(lb_logits):
    lg = lb_logits.astype(F32)
    neg_log_lb = -jax.nn.log_softmax(lg, axis=0)[0]
    return HG_CHUNK * jnp.max(neg_log_lb) > MAX_FAST_EXPONENT


def _ret_constants():
    lg = np.asarray(RET_LOG_DECAY, np.float64)
    idx = np.arange(RET_CHUNK, dtype=np.float64)
    diff = idx[:, None] - idx[None, :]
    dmask = np.where(diff >= 0, np.exp(np.where(diff >= 0, diff, 0.0)[None] * lg[:, None, None]), 0.0)
    zeta = np.exp((RET_CHUNK - 1 - idx)[None, :] * lg[:, None])
    xi = np.exp((idx + 1)[None, :] * lg[:, None])
    bc = lambda a: np.broadcast_to(a[:, :, None], (RET_HEADS, RET_CHUNK, RET_DK))
    return jnp.asarray(dmask, F32), jnp.asarray(bc(zeta), F32), jnp.asarray(bc(xi), F32)


def _chunk_tril(tm):
    i = np.arange(tm)
    m = (i[:, None] >= i[None, :]) & (i[:, None] // HG_CHUNK == i[None, :] // HG_CHUNK)
    return jnp.asarray(m, BF16)


def _fused_call(x2d, pos3, mod3, w_in_slabs, w_out_slabs, ln_w, ln_b, lb_logits, hg_norm_w, *, seq, safe):
    m, d = x2d.shape
    tm = min(TOKEN_TILE, seq)
    steps_per_batch = seq // tm
    n_tiles = m // tm
    dmask, zeta, xi = _ret_constants()
    tribd = _chunk_tril(tm)
    half = RET_DK // 2
    freqs = ROPE_BASE ** (-jnp.arange(half, dtype=F32) / half)
    freq2 = jnp.concatenate([freqs, freqs]).reshape(1, RET_DK)
    sign = jnp.concatenate([-jnp.ones((half,), F32), jnp.ones((half,), F32)]).reshape(1, RET_DK)

    last = n_tiles - 1
    cur = lambda s: jnp.minimum(s, last)
    mixd = lambda s: jnp.clip(s - 1, 0, last)
    done = lambda s: jnp.maximum(s - 2, 0)
    resident = lambda a: pl.BlockSpec(a.shape, lambda s: (0,) * a.ndim, pipeline_mode=pl.Buffered(1))
    return pl.pallas_call(
        functools.partial(_fused_kernel, tm=tm, steps_per_batch=steps_per_batch, n_tiles=n_tiles, safe=safe),
        grid=(n_tiles + 2,),
        in_specs=[pl.BlockSpec(memory_space=pl.ANY),
                  pl.BlockSpec((1, tm, 1), lambda s: (mixd(s) // steps_per_batch, mixd(s) % steps_per_batch, 0)),
                  pl.BlockSpec((1, 1, d), lambda s: (cur(s) // steps_per_batch, 0, 0)),
                  pl.BlockSpec((1, 1, d), lambda s: (cur(s) // steps_per_batch, 0, 1)),
                  pl.BlockSpec((1, 1, d), lambda s: (done(s) // steps_per_batch, 0, 2)),
                  resident(w_in_slabs), resident(w_out_slabs), resident(ln_w), resident(ln_b),
                  resident(lb_logits), resident(hg_norm_w), resident(dmask), resident(zeta),
                  resident(xi), resident(tribd), resident(freq2), resident(sign)],
        out_specs=pl.BlockSpec((tm, d), lambda s: (done(s), 0), pipeline_mode=pl.Buffered(1)),
        out_shape=jax.ShapeDtypeStruct((m, d), F32),
        scratch_shapes=[pltpu.VMEM((RET_HEADS, tm, RET_GROUP), F32),
                        pltpu.VMEM((HG_HEADS, tm, HG_GROUP), F32),
                        pltpu.VMEM((RET_HEADS, RET_DK, RET_DV), F32),
                        pltpu.VMEM((HG_HEADS, HG_DV, HG_DK), F32),
                        pltpu.VMEM((tm, MIX_WIDTH), BF16),
                        pltpu.VMEM((tm, d), F32),
                        pltpu.VMEM((tm, d), F32),
                        pltpu.VMEM((tm, d), BF16),
                        pltpu.SemaphoreType.DMA(()),
                        pltpu.SemaphoreType.DMA(())],
        compiler_params=pltpu.CompilerParams(dimension_semantics=("arbitrary",),
                                             vmem_limit_bytes=VMEM_LIMIT_BYTES),
        name="fused_layer_safe" if safe else "fused_layer",
    )(x2d, pos3, mod3, mod3, mod3, w_in_slabs, w_out_slabs, ln_w, ln_b,
      lb_logits, hg_norm_w, dmask, zeta, xi, tribd, freq2, sign)


def kernel(x, c, positions, w_ada, b_ada, w_in, lb_logits, hg_norm_w, w_out, ln_w, ln_b):
    bsz, seq, d = x.shape
    assert w_ada.shape[0] == DEPTH and seq % RET_CHUNK == 0 and d % PIECE == 0
    mod = _ada_call(c, w_ada[0], b_ada[0])
    mod3 = mod.reshape(bsz, 1, 3 * d)
    operands = (x.reshape(bsz * seq, d), positions.reshape(bsz, seq, 1), mod3,
                _column_slabs(w_in[0].astype(BF16)), _column_slabs(w_out[0].astype(BF16)),
                ln_w[0].reshape(1, d), ln_b[0].reshape(1, d),
                lb_logits.astype(F32), hg_norm_w[0].reshape(1, HG_WIDTH))
    out = lax.cond(_needs_safe_path(lb_logits),
                   lambda ops: _fused_call(*ops, seq=seq, safe=True),
                   lambda ops: _fused_call(*ops, seq=seq, safe=False),
                   operands)
    return out.reshape(bsz, seq, d)
```

```python
import functools

import numpy as np
import jax
import jax.numpy as jnp
from jax import lax
from jax.experimental import pallas as pl
from jax.experimental.pallas import tpu as pltpu

RET_HEADS = 4
RET_DK = 128
RET_DV = 256
HG_HEADS = 8
HG_DK = 128
HG_DV = 128
RET_CHUNK = 128
HG_CHUNK = 64
ROPE_BASE = 10000.0
EPS = 1e-6
DEPTH = 1
ALPHA = (2.0 * DEPTH) ** 0.25
RET_LOG2_DECAY = tuple(5.0 + 7.0 * h / (RET_HEADS - 1) for h in range(RET_HEADS))
RET_LOG_DECAY = tuple(float(np.log1p(-np.exp2(-e))) for e in RET_LOG2_DECAY)

RET_QK = RET_HEADS * RET_DK
RET_WIDTH = RET_HEADS * RET_DV
HG_QK = HG_HEADS * HG_DK
HG_WIDTH = HG_HEADS * HG_DV
MIX_WIDTH = RET_WIDTH + HG_WIDTH
OFF_RQ = 0
OFF_RK = OFF_RQ + RET_QK
OFF_RV = OFF_RK + RET_QK
OFF_RG = OFF_RV + RET_WIDTH
OFF_HQ = OFF_RG + RET_WIDTH
OFF_HF = OFF_HQ + HG_QK
OFF_HI = OFF_HF + HG_QK
OFF_HG = OFF_HI + HG_WIDTH
IN_WIDTH = OFF_HG + HG_WIDTH
RET_GROUP = 2 * RET_DK + 2 * RET_DV
HG_GROUP = 2 * HG_DK + 2 * HG_DV

F32 = jnp.float32
BF16 = jnp.bfloat16
TOKEN_TILE = 256
PIECE = 256
V7X_VMEM_BYTES = 64 * 1024 * 1024
VMEM_LIMIT_BYTES = V7X_VMEM_BYTES - 512 * 1024
HEAD_ORDER = (("ret", 0), ("ret", 1), ("hg", 0), ("hg", 1), ("hg", 2), ("hg", 3),
              ("ret", 2), ("ret", 3), ("hg", 4), ("hg", 5), ("hg", 6), ("hg", 7))
MAX_FAST_EXPONENT = 80.0
LN_ROWS = 16
LN_BLOCKS_PER_PIECE = 4


def _column_slabs(w):
    k, n = w.shape
    return w.reshape(k, n // PIECE, PIECE).transpose(1, 0, 2)


def _sigmoid(x):
    return 1.0 / (1.0 + jnp.exp(-x))


def _ada_kernel(c_ref, w_ref, b_ref, o_ref):
    c = c_ref[...]
    cond = c * _sigmoid(c)
    o_ref[...] = jnp.dot(cond, w_ref[...], preferred_element_type=F32) + b_ref[...]


def _ada_call(c, w, b):
    bsz, d = c.shape
    n = w.shape[1]
    tn = 512 if n % 512 == 0 else n
    return pl.pallas_call(
        _ada_kernel,
        grid=(n // tn,),
        in_specs=[pl.BlockSpec((bsz, d), lambda j: (0, 0)),
                  pl.BlockSpec((d, tn), lambda j: (0, j)),
                  pl.BlockSpec((1, tn), lambda j: (0, j))],
        out_specs=pl.BlockSpec((bsz, tn), lambda j: (0, j)),
        out_shape=jax.ShapeDtypeStruct((bsz, n), F32),
        compiler_params=pltpu.CompilerParams(dimension_semantics=("arbitrary",)),
        name="ada_mod",
    )(c, w, b.reshape(1, n))


def _dot(a, b):
    return jnp.dot(a, b, preferred_element_type=F32)


def _dot_nt(a, b):
    return lax.dot_general(a, b, (((1,), (1,)), ((), ())), preferred_element_type=F32)


def _dot_tn(a, b):
    return lax.dot_general(a, b, (((0,), (0,)), ((), ())), preferred_element_type=F32)


def _split3(x):
    hi = x.astype(BF16)
    r1 = x - hi.astype(F32)
    mid = r1.astype(BF16)
    lo = (r1 - mid.astype(F32)).astype(BF16)
    return hi, mid, lo


def _layernorm(x):
    mu = jnp.mean(x, axis=-1, keepdims=True)
    xc = x - mu
    var = jnp.mean(xc * xc, axis=-1, keepdims=True)
    return xc * lax.rsqrt(var + EPS)


def _rmsnorm(x):
    return x * lax.rsqrt(jnp.mean(x * x, axis=-1, keepdims=True) + EPS)


def _fused_kernel(xhbm_ref, pos_ref, shift_ref, scale_ref, gate_ref, win_ref, wout_ref, lnw_ref,
                  lnb_ref, lbl_ref, hgw_ref, dmask_ref, zeta_ref, xi_ref, tribd_ref, freq_ref, sign_ref,
                  o_ref, pret_ref, phg_ref, sret_ref, shg_ref, mixed_ref, xin_ref, xres_ref, h_ref,
                  in_sem, res_sem, *, tm, steps_per_batch, n_tiles, safe):
    s = pl.program_id(0)
    n_ret_chunks = tm // RET_CHUNK
    n_hg_chunks = tm // HG_CHUNK

    def residual_copy(step):
        tile = jnp.maximum(step - 2, 0)
        return pltpu.make_async_copy(xhbm_ref.at[pl.ds(tile * tm, tm), :], xres_ref, res_sem)

    def input_copy(step):
        tile = jnp.minimum(step, n_tiles - 1)
        return pltpu.make_async_copy(xhbm_ref.at[pl.ds(tile * tm, tm), :], xin_ref, in_sem)

    @pl.when(s == 0)
    def _():
        pret_ref[...] = jnp.zeros_like(pret_ref)
        phg_ref[...] = jnp.zeros_like(phg_ref)
        mixed_ref[...] = jnp.zeros_like(mixed_ref)
        residual_copy(s).start()
        input_copy(s).start()

    @pl.when(lax.rem(jnp.maximum(s - 1, 0), steps_per_batch) == 0)
    def _():
        sret_ref[...] = jnp.zeros_like(sret_ref)
        shg_ref[...] = jnp.zeros_like(shg_ref)

    residual_copy(s).wait()
    input_copy(s).wait()

    prev_q = []
    pending = []

    def fill(n=1):
        for _ in range(n):
            while prev_q or pending:
                is_matmul, thunk = (prev_q or pending).pop(0)
                thunk()
                if is_matmul:
                    break

    def finish_previous_tile():
        while prev_q:
            fill()

    mixed_heads = set()

    def release(kind, h):
        mixed_heads.add((kind, h))
        if (kind, h ^ 1) in mixed_heads:
            pending.extend((True, p) for p in proj_pieces(kind, h & ~1))

    def proj_pieces(kind, h0):
        dst_ref = pret_ref if kind == "ret" else phg_ref

        def piece(src_col, dests):
            def run():
                r = _dot(h_ref[...], win_ref[src_col // PIECE])
                c = 0
                for head, col, width in dests:
                    dst_ref[head, :, pl.ds(col, width)] = r[:, c:c + width]
                    c += width
            return run

        if kind == "ret":
            pieces = [piece(off + h0 * RET_DK, [(h0, col, RET_DK), (h0 + 1, col, RET_DK)])
                      for off, col in ((OFF_RQ, 0), (OFF_RK, RET_DK))]
            pieces += [piece(off + h * RET_DV, [(h, col, RET_DV)])
                       for off, col in ((OFF_RV, 2 * RET_DK), (OFF_RG, 2 * RET_DK + RET_DV))
                       for h in (h0, h0 + 1)]
            return pieces
        return [piece(off + h0 * HG_DK, [(h0, col, HG_DK), (h0 + 1, col, HG_DK)])
                for off, col in ((OFF_HQ, 0), (OFF_HF, HG_DK), (OFF_HI, 2 * HG_DK), (OFF_HG, 2 * HG_DK + HG_DV))]

    def outproj_piece(c0):
        def run():
            o_ref[:, pl.ds(c0, PIECE)] = _dot(mixed_ref[...], wout_ref[c0 // PIECE])
        return run

    def final_layernorm():
        for r0 in range(0, tm, LN_ROWS):
            rows = pl.ds(r0, LN_ROWS)
            z = ALPHA * xres_ref[rows, :] + gate_ref[0] * o_ref[rows, :]
            o_ref[rows, :] = _layernorm(z) * lnw_ref[...] + lnb_ref[...]

        @pl.when(s + 1 < pl.num_programs(0))
        def _():
            residual_copy(s + 1).start()

    prev_q.extend((True, outproj_piece(c0)) for c0 in range(0, o_ref.shape[1], PIECE))
    prev_q.append((False, final_layernorm))

    for i, r0 in enumerate(range(0, tm, LN_ROWS)):
        rows = pl.ds(r0, LN_ROWS)
        h_ref[rows, :] = (_layernorm(xin_ref[rows, :]) * (1.0 + scale_ref[0]) + shift_ref[0]).astype(BF16)
        if i % LN_BLOCKS_PER_PIECE == LN_BLOCKS_PER_PIECE - 1:
            fill()

    @pl.when(s + 1 < pl.num_programs(0))
    def _():
        input_copy(s + 1).start()

    ang = pos_ref[0].astype(F32) * freq_ref[...]
    cosf = jnp.cos(ang)
    sinf = jnp.sin(ang) * sign_ref[...]
    fill()

    def mix_ret(h):
        cdec = float(np.exp(RET_CHUNK * RET_LOG_DECAY[h]))
        scores, kv, qx, vs = [], [], [], []
        for c in range(n_ret_chunks):
            rows = pl.ds(c * RET_CHUNK, RET_CHUNK)
            cf = cosf[c * RET_CHUNK:(c + 1) * RET_CHUNK]
            sf = sinf[c * RET_CHUNK:(c + 1) * RET_CHUNK]
            q = pret_ref[h, rows, pl.ds(0, RET_DK)]
            k = pret_ref[h, rows, pl.ds(RET_DK, RET_DK)]
            v = pret_ref[h, rows, pl.ds(2 * RET_DK, RET_DV)].astype(BF16)
            q = (q * cf + pltpu.roll(q, RET_DK // 2, 1) * sf) * (RET_DK ** -0.5)
            k = k * cf + pltpu.roll(k, RET_DK // 2, 1) * sf
            scores.append(_dot_nt(q.astype(BF16), k.astype(BF16)))
            kv.append(_dot_tn((k * zeta_ref[h]).astype(BF16), v))
            qx.append((q * xi_ref[h]).astype(BF16))
            vs.append(v)
        fill()
        state = sret_ref[h]
        rets = []
        for c in range(n_ret_chunks):
            p = (scores[c] * dmask_ref[h]).astype(BF16)
            rets.append(_dot(p, vs[c]) + _dot(qx[c], state.astype(BF16)))
            state = cdec * state + kv[c]
        sret_ref[h] = state
        fill()
        finish_previous_tile()
        for c in range(n_ret_chunks):
            rows = pl.ds(c * RET_CHUNK, RET_CHUNK)
            g = pret_ref[h, rows, pl.ds(2 * RET_DK + RET_DV, RET_DV)]
            mixed_ref[rows, pl.ds(h * RET_DV, RET_DV)] = (_rmsnorm(rets[c]) * (g * _sigmoid(g))).astype(BF16)
        fill()
        release("ret", h)

    l0 = lbl_ref[0:1, :]
    l1 = lbl_ref[1:2, :]
    lmax = jnp.maximum(l0, l1)
    e0 = jnp.exp(l0 - lmax)
    lb_all = e0 / (e0 + jnp.exp(l1 - lmax))
    row = lax.broadcasted_iota(jnp.int32, (HG_CHUNK, HG_CHUNK), 0)
    col = lax.broadcasted_iota(jnp.int32, (HG_CHUNK, HG_CHUNK), 1)
    causal = row >= col

    def intra_chunk_scores_safe(q, k, bc):
        trow = lax.broadcasted_iota(jnp.int32, (HG_CHUNK, HG_DK), 0)
        xor = jnp.bitwise_xor(row, col)
        acc = jnp.where(row == col, jnp.sum(q * k, axis=-1, keepdims=True), 0.0)
        half = 1
        while half < HG_CHUNK:
            block = 2 * half
            upper = jnp.bitwise_and(trow, half) != 0
            if block >= 8:
                b3 = bc.reshape(HG_CHUNK // block, block, HG_DK)
                bm = jnp.broadcast_to(b3[:, half - 1:half, :], b3.shape).reshape(HG_CHUNK, HG_DK)
            elif half == 2:
                b3 = bc.reshape(HG_CHUNK // 8, 8, HG_DK)
                sub = lax.broadcasted_iota(jnp.int32, b3.shape, 1)
                bm = jnp.where(sub < 4, jnp.broadcast_to(b3[:, 1:2, :], b3.shape),
                               jnp.broadcast_to(b3[:, 5:6, :], b3.shape)).reshape(HG_CHUNK, HG_DK)
            else:
                bm = jnp.where(upper, pltpu.roll(bc, 1, 0), bc)
            u = (jnp.where(upper, q, k) * jnp.exp(-jnp.abs(bc - bm))).astype(BF16)
            level = (row > col) & (xor >= half) & (xor < block)
            acc = jnp.where(level, _dot_nt(u, u), acc)
            half = block
        return acc

    def mix_hg(h):
        lb = lb_all[:, h * HG_DK:(h + 1) * HG_DK]
        fp = phg_ref[h, :, pl.ds(HG_DK, HG_DK)]
        sg = _sigmoid(fp)
        logf = jnp.log(lb + (1.0 - lb) * sg)
        kk = (1.0 - lb) * (1.0 - sg)
        parts = _dot(tribd_ref[...], jnp.concatenate(_split3(logf), axis=1))
        b = parts[:, 0:HG_DK] + parts[:, HG_DK:2 * HG_DK] + parts[:, 2 * HG_DK:3 * HG_DK]
        fill()
        hq = phg_ref[h, :, pl.ds(0, HG_DK)]
        qd = (hq * jnp.exp(b)).astype(BF16)
        if not safe:
            kd = (kk * jnp.exp(-b)).astype(BF16)
        attn, kvt, elast, vs = [], [], [], []
        for c in range(n_hg_chunks):
            r0, r1 = c * HG_CHUNK, (c + 1) * HG_CHUNK
            b_last = b[r1 - 1:r1, :]
            ks = (kk[r0:r1] * jnp.exp(b_last - b[r0:r1])).astype(BF16)
            v = phg_ref[h, pl.ds(r0, HG_CHUNK), pl.ds(2 * HG_DK, HG_DV)].astype(BF16)
            if safe:
                attn.append(intra_chunk_scores_safe(hq[r0:r1], kk[r0:r1], b[r0:r1]))
            else:
                attn.append(jnp.where(causal, _dot_nt(qd[r0:r1], kd[r0:r1]), 0.0))
            kvt.append(_dot_tn(v, ks))
            elast.append(jnp.exp(b_last))
            vs.append(v)
        fill()
        state_t = shg_ref[h]
        outs = []
        for c in range(n_hg_chunks):
            r0, r1 = c * HG_CHUNK, (c + 1) * HG_CHUNK
            outs.append(_dot(attn[c].astype(BF16), vs[c]) + _dot_nt(qd[r0:r1], state_t.astype(BF16)))
            state_t = state_t * elast[c] + kvt[c]
        shg_ref[h] = state_t
        fill()
        finish_previous_tile()
        hgw = hgw_ref[:, pl.ds(h * HG_DV, HG_DV)]
        for c in range(n_hg_chunks):
            rows = pl.ds(c * HG_CHUNK, HG_CHUNK)
            g = phg_ref[h, rows, pl.ds(2 * HG_DK + HG_DV, HG_DV)]
            mixed_ref[rows, pl.ds(RET_WIDTH + h * HG_DV, HG_DV)] = (
                _rmsnorm(outs[c]) * hgw * (g * _sigmoid(g))).astype(BF16)
        fill()
        release("hg", h)

    for kind, h in HEAD_ORDER:
        if kind == "ret":
            mix_ret(h)
        else:
            mix_hg(h)
    fill(len(pending))


def _needs_safe_path(lb_logits):
    lg = lb_logits.astype(F32)
    neg_log_lb = -jax.nn.log_softmax(lg, axis=0)[0]
    return HG_CHUNK * jnp.max(neg_log_lb) > MAX_FAST_EXPONENT


def _ret_constants():
    lg = np.asarray(RET_LOG_DECAY, np.float64)
    idx = np.arange(RET_CHUNK, dtype=np.float64)
    diff = idx[:, None] - idx[None, :]
    dmask = np.where(diff >= 0, np.exp(np.where(diff >= 0, diff, 0.0)[None] * lg[:, None, None]), 0.0)
    zeta = np.exp((RET_CHUNK - 1 - idx)[None, :] * lg[:, None])
    xi = np.exp((idx + 1)[None, :] * lg[:, None])
    bc = lambda a: np.broadcast_to(a[:, :, None], (RET_HEADS, RET_CHUNK, RET_DK))
    return jnp.asarray(dmask, F32), jnp.asarray(bc(zeta), F32), jnp.asarray(bc(xi), F32)


def _chunk_tril(tm):
    i = np.arange(tm)
    m = (i[:, None] >= i[None, :]) & (i[:, None] // HG_CHUNK == i[None, :] // HG_CHUNK)
    return jnp.asarray(m, BF16)


def _fused_call(x2d, pos3, mod3, w_in_slabs, w_out_slabs, ln_w, ln_b, lb_logits, hg_norm_w, *, seq, safe):
    m, d = x2d.shape
    tm = min(TOKEN_TILE, seq)
    steps_per_batch = seq // tm
    n_tiles = m // tm
    dmask, zeta, xi = _ret_constants()
    tribd = _chunk_tril(tm)
    half = RET_DK // 2
    freqs = ROPE_BASE ** (-jnp.arange(half, dtype=F32) / half)
    freq2 = jnp.concatenate([freqs, freqs]).reshape(1, RET_DK)
    sign = jnp.concatenate([-jnp.ones((half,), F32), jnp.ones((half,), F32)]).reshape(1, RET_DK)

    last = n_tiles - 1
    cur = lambda s: jnp.minimum(s, last)
    mixd = lambda s: jnp.clip(s - 1, 0, last)
    done = lambda s: jnp.maximum(s - 2, 0)
    resident = lambda a: pl.BlockSpec(a.shape, lambda s: (0,) * a.ndim, pipeline_mode=pl.Buffered(1))
    return pl.pallas_call(
        functools.partial(_fused_kernel, tm=tm, steps_per_batch=steps_per_batch, n_tiles=n_tiles, safe=safe),
        grid=(n_tiles + 2,),
        in_specs=[pl.BlockSpec(memory_space=pl.ANY),
                  pl.BlockSpec((1, tm, 1), lambda s: (mixd(s) // steps_per_batch, mixd(s) % steps_per_batch, 0)),
                  pl.BlockSpec((1, 1, d), lambda s: (cur(s) // steps_per_batch, 0, 0)),
                  pl.BlockSpec((1, 1, d), lambda s: (cur(s) // steps_per_batch, 0, 1)),
                  pl.BlockSpec((1, 1, d), lambda s: (done(s) // steps_per_batch, 0, 2)),
                  resident(w_in_slabs), resident(w_out_slabs), resident(ln_w), resident(ln_b),
                  resident(lb_logits), resident(hg_norm_w), resident(dmask), resident(zeta),
                  resident(xi), resident(tribd), resident(freq2), resident(sign)],
        out_specs=pl.BlockSpec((tm, d), lambda s: (done(s), 0), pipeline_mode=pl.Buffered(1)),
        out_shape=jax.ShapeDtypeStruct((m, d), F32),
        scratch_shapes=[pltpu.VMEM((RET_HEADS, tm, RET_GROUP), F32),
                        pltpu.VMEM((HG_HEADS, tm, HG_GROUP), F32),
                        pltpu.VMEM((RET_HEADS, RET_DK, RET_DV), F32),
                        pltpu.VMEM((HG_HEADS, HG_DV, HG_DK), F32),
                        pltpu.VMEM((tm, MIX_WIDTH), BF16),
                        pltpu.VMEM((tm, d), F32),
                        pltpu.VMEM((tm, d), F32),
                        pltpu.VMEM((tm, d), BF16),
                        pltpu.SemaphoreType.DMA(()),
                        pltpu.SemaphoreType.DMA(())],
        compiler_params=pltpu.CompilerParams(dimension_semantics=("arbitrary",),
                                             vmem_limit_bytes=VMEM_LIMIT_BYTES),
        name="fused_layer_safe" if safe else "fused_layer",
    )(x2d, pos3, mod3, mod3, mod3, w_in_slabs, w_out_slabs, ln_w, ln_b,
      lb_logits, hg_norm_w, dmask, zeta, xi, tribd, freq2, sign)


def kernel(x, c, positions, w_ada, b_ada, w_in, lb_logits, hg_norm_w, w_out, ln_w, ln_b):
    bsz, seq, d = x.shape
    assert w_ada.shape[0] == DEPTH and seq % RET_CHUNK == 0 and d % PIECE == 0
    mod = _ada_call(c, w_ada[0], b_ada[0])
    mod3 = mod.reshape(bsz, 1, 3 * d)
    operands = (x.reshape(bsz * seq, d), positions.reshape(bsz, seq, 1), mod3, w_in[0], w_out[0],
                ln_w[0].reshape(1, d), ln_b[0].reshape(1, d),
                lb_logits.astype(F32), hg_norm_w[0].reshape(1, HG_WIDTH))

    def run(safe, x2d, pos3, mod3_, w_in_f32, w_out_f32, *rest):
        return _fused_call(x2d, pos3, mod3_, _column_slabs(w_in_f32.astype(BF16)),
                           _column_slabs(w_out_f32.astype(BF16)), *rest, seq=seq, safe=safe)

    out = lax.cond(_needs_safe_path(lb_logits),
                   lambda ops: run(True, *ops), lambda ops: run(False, *ops), operands)
    return out.reshape(bsz, seq, d)
```

```python
import functools

import numpy as np
import jax
import jax.numpy as jnp
from jax import lax
from jax.experimental import pallas as pl
from jax.experimental.pallas import tpu as pltpu

RET_HEADS = 4
RET_DK = 128
RET_DV = 256
HG_HEADS = 8
HG_DK = 128
HG_DV = 128
RET_CHUNK = 128
HG_CHUNK = 64
ROPE_BASE = 10000.0
EPS = 1e-6
DEPTH = 1
ALPHA = (2.0 * DEPTH) ** 0.25
RET_LOG2_DECAY = tuple(5.0 + 7.0 * h / (RET_HEADS - 1) for h in range(RET_HEADS))
RET_LOG_DECAY = tuple(float(np.log1p(-np.exp2(-e))) for e in RET_LOG2_DECAY)

RET_QK = RET_HEADS * RET_DK
RET_WIDTH = RET_HEADS * RET_DV
HG_QK = HG_HEADS * HG_DK
HG_WIDTH = HG_HEADS * HG_DV
MIX_WIDTH = RET_WIDTH + HG_WIDTH
OFF_RQ = 0
OFF_RK = OFF_RQ + RET_QK
OFF_RV = OFF_RK + RET_QK
OFF_RG = OFF_RV + RET_WIDTH
OFF_HQ = OFF_RG + RET_WIDTH
OFF_HF = OFF_HQ + HG_QK
OFF_HI = OFF_HF + HG_QK
OFF_HG = OFF_HI + HG_WIDTH
IN_WIDTH = OFF_HG + HG_WIDTH
RET_GROUP = 2 * RET_DK + 2 * RET_DV
HG_GROUP = 2 * HG_DK + 2 * HG_DV

F32 = jnp.float32
BF16 = jnp.bfloat16
TOKEN_TILE = 256
PIECE = 256
V7X_VMEM_BYTES = 64 * 1024 * 1024
VMEM_LIMIT_BYTES = V7X_VMEM_BYTES - 512 * 1024
HEAD_ORDER = (("ret", 0), ("ret", 1), ("hg", 0), ("hg", 1), ("hg", 2), ("hg", 3),
              ("ret", 2), ("ret", 3), ("hg", 4), ("hg", 5), ("hg", 6), ("hg", 7))
MAX_FAST_EXPONENT = 80.0
LN_ROWS = 16
LN_BLOCKS_PER_PIECE = 4


def _slab_kernel(w_ref, o_ref):
    for j in range(o_ref.shape[0]):
        o_ref[j] = w_ref[:, pl.ds(j * PIECE, PIECE)].astype(BF16)


def _column_slabs(w):
    k, n = w.shape
    per_step = 2 if (n // PIECE) % 2 == 0 else 1
    return pl.pallas_call(
        _slab_kernel,
        grid=(n // (PIECE * per_step),),
        in_specs=[pl.BlockSpec((k, PIECE * per_step), lambda j: (0, j))],
        out_specs=pl.BlockSpec((per_step, k, PIECE), lambda j: (j, 0, 0)),
        out_shape=jax.ShapeDtypeStruct((n // PIECE, k, PIECE), BF16),
        compiler_params=pltpu.CompilerParams(dimension_semantics=("arbitrary",)),
        name="weight_slabs",
    )(w)


def _sigmoid(x):
    return 1.0 / (1.0 + jnp.exp(-x))


def _ada_kernel(c_ref, w_ref, b_ref, o_ref):
    c = c_ref[...]
    cond = c * _sigmoid(c)
    o_ref[...] = jnp.dot(cond, w_ref[...], preferred_element_type=F32) + b_ref[...]


def _ada_call(c, w, b):
    bsz, d = c.shape
    n = w.shape[1]
    tn = 512 if n % 512 == 0 else n
    return pl.pallas_call(
        _ada_kernel,
        grid=(n // tn,),
        in_specs=[pl.BlockSpec((bsz, d), lambda j: (0, 0)),
                  pl.BlockSpec((d, tn), lambda j: (0, j)),
                  pl.BlockSpec((1, tn), lambda j: (0, j))],
        out_specs=pl.BlockSpec((bsz, tn), lambda j: (0, j)),
        out_shape=jax.ShapeDtypeStruct((bsz, n), F32),
        compiler_params=pltpu.CompilerParams(dimension_semantics=("arbitrary",)),
        name="ada_mod",
    )(c, w, b.reshape(1, n))


def _dot(a, b):
    return jnp.dot(a, b, preferred_element_type=F32)


def _dot_nt(a, b):
    return lax.dot_general(a, b, (((1,), (1,)), ((), ())), preferred_element_type=F32)


def _dot_tn(a, b):
    return lax.dot_general(a, b, (((0,), (0,)), ((), ())), preferred_element_type=F32)


def _split3(x):
    hi = x.astype(BF16)
    r1 = x - hi.astype(F32)
    mid = r1.astype(BF16)
    lo = (r1 - mid.astype(F32)).astype(BF16)
    return hi, mid, lo


def _layernorm(x):
    mu = jnp.mean(x, axis=-1, keepdims=True)
    xc = x - mu
    var = jnp.mean(xc * xc, axis=-1, keepdims=True)
    return xc * lax.rsqrt(var + EPS)


def _rmsnorm(x):
    return x * lax.rsqrt(jnp.mean(x * x, axis=-1, keepdims=True) + EPS)


def _fused_kernel(xhbm_ref, pos_ref, shift_ref, scale_ref, gate_ref, win_ref, wout_ref, lnw_ref,
                  lnb_ref, lbl_ref, hgw_ref, dmask_ref, zeta_ref, xi_ref, tribd_ref, freq_ref, sign_ref,
                  o_ref, pret_ref, phg_ref, sret_ref, shg_ref, mixed_ref, xin_ref, xres_ref, h_ref,
                  in_sem, res_sem, *, tm, steps_per_batch, n_tiles, safe):
    s = pl.program_id(0)
    n_ret_chunks = tm // RET_CHUNK
    n_hg_chunks = tm // HG_CHUNK

    def residual_copy(step):
        tile = jnp.maximum(step - 2, 0)
        return pltpu.make_async_copy(xhbm_ref.at[pl.ds(tile * tm, tm), :], xres_ref, res_sem)

    def input_copy(step):
        tile = jnp.minimum(step, n_tiles - 1)
        return pltpu.make_async_copy(xhbm_ref.at[pl.ds(tile * tm, tm), :], xin_ref, in_sem)

    @pl.when(s == 0)
    def _():
        pret_ref[...] = jnp.zeros_like(pret_ref)
        phg_ref[...] = jnp.zeros_like(phg_ref)
        mixed_ref[...] = jnp.zeros_like(mixed_ref)
        residual_copy(s).start()
        input_copy(s).start()

    @pl.when(lax.rem(jnp.maximum(s - 1, 0), steps_per_batch) == 0)
    def _():
        sret_ref[...] = jnp.zeros_like(sret_ref)
        shg_ref[...] = jnp.zeros_like(shg_ref)

    residual_copy(s).wait()
    input_copy(s).wait()

    prev_q = []
    pending = []

    def fill(n=1):
        for _ in range(n):
            while prev_q or pending:
                is_matmul, thunk = (prev_q or pending).pop(0)
                thunk()
                if is_matmul:
                    break

    def finish_previous_tile():
        while prev_q:
            fill()

    mixed_heads = set()

    def release(kind, h):
        mixed_heads.add((kind, h))
        if (kind, h ^ 1) in mixed_heads:
            pending.extend((True, p) for p in proj_pieces(kind, h & ~1))

    def proj_pieces(kind, h0):
        dst_ref = pret_ref if kind == "ret" else phg_ref

        def piece(src_col, dests):
            def run():
                r = _dot(h_ref[...], win_ref[src_col // PIECE])
                c = 0
                for head, col, width in dests:
                    dst_ref[head, :, pl.ds(col, width)] = r[:, c:c + width]
                    c += width
            return run

        if kind == "ret":
            pieces = [piece(off + h0 * RET_DK, [(h0, col, RET_DK), (h0 + 1, col, RET_DK)])
                      for off, col in ((OFF_RQ, 0), (OFF_RK, RET_DK))]
            pieces += [piece(off + h * RET_DV, [(h, col, RET_DV)])
                       for off, col in ((OFF_RV, 2 * RET_DK), (OFF_RG, 2 * RET_DK + RET_DV))
                       for h in (h0, h0 + 1)]
            return pieces
        return [piece(off + h0 * HG_DK, [(h0, col, HG_DK), (h0 + 1, col, HG_DK)])
                for off, col in ((OFF_HQ, 0), (OFF_HF, HG_DK), (OFF_HI, 2 * HG_DK), (OFF_HG, 2 * HG_DK + HG_DV))]

    def outproj_piece(c0):
        def run():
            o_ref[:, pl.ds(c0, PIECE)] = _dot(mixed_ref[...], wout_ref[c0 // PIECE])
        return run

    def final_layernorm():
        for r0 in range(0, tm, LN_ROWS):
            rows = pl.ds(r0, LN_ROWS)
            z = ALPHA * xres_ref[rows, :] + gate_ref[0] * o_ref[rows, :]
            o_ref[rows, :] = _layernorm(z) * lnw_ref[...] + lnb_ref[...]

        @pl.when(s + 1 < pl.num_programs(0))
        def _():
            residual_copy(s + 1).start()

    prev_q.extend((True, outproj_piece(c0)) for c0 in range(0, o_ref.shape[1], PIECE))
    prev_q.append((False, final_layernorm))

    for i, r0 in enumerate(range(0, tm, LN_ROWS)):
        rows = pl.ds(r0, LN_ROWS)
        h_ref[rows, :] = (_layernorm(xin_ref[rows, :]) * (1.0 + scale_ref[0]) + shift_ref[0]).astype(BF16)
        if i % LN_BLOCKS_PER_PIECE == LN_BLOCKS_PER_PIECE - 1:
            fill()

    @pl.when(s + 1 < pl.num_programs(0))
    def _():
        input_copy(s + 1).start()

    ang = pos_ref[0].astype(F32) * freq_ref[...]
    cosf = jnp.cos(ang)
    sinf = jnp.sin(ang) * sign_ref[...]
    fill()

    def mix_ret(h):
        cdec = float(np.exp(RET_CHUNK * RET_LOG_DECAY[h]))
        scores, kv, qx, vs = [], [], [], []
        for c in range(n_ret_chunks):
            rows = pl.ds(c * RET_CHUNK, RET_CHUNK)
            cf = cosf[c * RET_CHUNK:(c + 1) * RET_CHUNK]
            sf = sinf[c * RET_CHUNK:(c + 1) * RET_CHUNK]
            q = pret_ref[h, rows, pl.ds(0, RET_DK)]
            k = pret_ref[h, rows, pl.ds(RET_DK, RET_DK)]
            v = pret_ref[h, rows, pl.ds(2 * RET_DK, RET_DV)].astype(BF16)
            q = (q * cf + pltpu.roll(q, RET_DK // 2, 1) * sf) * (RET_DK ** -0.5)
            k = k * cf + pltpu.roll(k, RET_DK // 2, 1) * sf
            scores.append(_dot_nt(q.astype(BF16), k.astype(BF16)))
            kv.append(_dot_tn((k * zeta_ref[h]).astype(BF16), v))
            qx.append((q * xi_ref[h]).astype(BF16))
            vs.append(v)
        fill()
        state = sret_ref[h]
        rets = []
        for c in range(n_ret_chunks):
            p = (scores[c] * dmask_ref[h]).astype(BF16)
            rets.append(_dot(p, vs[c]) + _dot(qx[c], state.astype(BF16)))
            state = cdec * state + kv[c]
        sret_ref[h] = state
        fill()
        finish_previous_tile()
        for c in range(n_ret_chunks):
            rows = pl.ds(c * RET_CHUNK, RET_CHUNK)
            g = pret_ref[h, rows, pl.ds(2 * RET_DK + RET_DV, RET_DV)]
            mixed_ref[rows, pl.ds(h * RET_DV, RET_DV)] = (_rmsnorm(rets[c]) * (g * _sigmoid(g))).astype(BF16)
        fill()
        release("ret", h)

    l0 = lbl_ref[0:1, :]
    l1 = lbl_ref[1:2, :]
    lmax = jnp.maximum(l0, l1)
    e0 = jnp.exp(l0 - lmax)
    lb_all = e0 / (e0 + jnp.exp(l1 - lmax))
    row = lax.broadcasted_iota(jnp.int32, (HG_CHUNK, HG_CHUNK), 0)
    col = lax.broadcasted_iota(jnp.int32, (HG_CHUNK, HG_CHUNK), 1)
    causal = row >= col

    def intra_chunk_scores_safe(q, k, bc):
        trow = lax.broadcasted_iota(jnp.int32, (HG_CHUNK, HG_DK), 0)
        xor = jnp.bitwise_xor(row, col)
        acc = jnp.where(row == col, jnp.sum(q * k, axis=-1, keepdims=True), 0.0)
        half = 1
        while half < HG_CHUNK:
            block = 2 * half
            upper = jnp.bitwise_and(trow, half) != 0
            if block >= 8:
                b3 = bc.reshape(HG_CHUNK // block, block, HG_DK)
                bm = jnp.broadcast_to(b3[:, half - 1:half, :], b3.shape).reshape(HG_CHUNK, HG_DK)
            elif half == 2:
                b3 = bc.reshape(HG_CHUNK // 8, 8, HG_DK)
                sub = lax.broadcasted_iota(jnp.int32, b3.shape, 1)
                bm = jnp.where(sub < 4, jnp.broadcast_to(b3[:, 1:2, :], b3.shape),
                               jnp.broadcast_to(b3[:, 5:6, :], b3.shape)).reshape(HG_CHUNK, HG_DK)
            else:
                bm = jnp.where(upper, pltpu.roll(bc, 1, 0), bc)
            u = (jnp.where(upper, q, k) * jnp.exp(-jnp.abs(bc - bm))).astype(BF16)
            level = (row > col) & (xor >= half) & (xor < block)
            acc = jnp.where(level, _dot_nt(u, u), acc)
            half = block
        return acc

    def mix_hg(h):
        lb = lb_all[:, h * HG_DK:(h + 1) * HG_DK]
        fp = phg_ref[h, :, pl.ds(HG_DK, HG_DK)]
        sg = _sigmoid(fp)
        logf = jnp.log(lb + (1.0 - lb) * sg)
        kk = (1.0 - lb) * (1.0 - sg)
        parts = _dot(tribd_ref[...], jnp.concatenate(_split3(logf), axis=1))
        b = parts[:, 0:HG_DK] + parts[:, HG_DK:2 * HG_DK] + parts[:, 2 * HG_DK:3 * HG_DK]
        fill()
        hq = phg_ref[h, :, pl.ds(0, HG_DK)]
        qd = (hq * jnp.exp(b)).astype(BF16)
        if not safe:
            kd = (kk * jnp.exp(-b)).astype(BF16)
        attn, kvt, elast, vs = [], [], [], []
        for c in range(n_hg_chunks):
            r0, r1 = c * HG_CHUNK, (c + 1) * HG_CHUNK
            b_last = b[r1 - 1:r1, :]
            ks = (kk[r0:r1] * jnp.exp(b_last - b[r0:r1])).astype(BF16)
            v = phg_ref[h, pl.ds(r0, HG_CHUNK), pl.ds(2 * HG_DK, HG_DV)].astype(BF16)
            if safe:
                attn.append(intra_chunk_scores_safe(hq[r0:r1], kk[r0:r1], b[r0:r1]))
            else:
                attn.append(jnp.where(causal, _dot_nt(qd[r0:r1], kd[r0:r1]), 0.0))
            kvt.append(_dot_tn(v, ks))
            elast.append(jnp.exp(b_last))
            vs.append(v)
        fill()
        state_t = shg_ref[h]
        outs = []
        for c in range(n_hg_chunks):
            r0, r1 = c * HG_CHUNK, (c + 1) * HG_CHUNK
            outs.append(_dot(attn[c].astype(BF16), vs[c]) + _dot_nt(qd[r0:r1], state_t.astype(BF16)))
            state_t = state_t * elast[c] + kvt[c]
        shg_ref[h] = state_t
        fill()
        finish_previous_tile()
        hgw = hgw_ref[:, pl.ds(h * HG_DV, HG_DV)]
        for c in range(n_hg_chunks):
            rows = pl.ds(c * HG_CHUNK, HG_CHUNK)
            g = phg_ref[h, rows, pl.ds(2 * HG_DK + HG_DV, HG_DV)]
            mixed_ref[rows, pl.ds(RET_WIDTH + h * HG_DV, HG_DV)] = (
                _rmsnorm(outs[c]) * hgw * (g * _sigmoid(g))).astype(BF16)
        fill()
        release("hg", h)

    for kind, h in HEAD_ORDER:
        if kind == "ret":
            mix_ret(h)
        else:
            mix_hg(h)
    fill(len(pending))


def _needs_safe_path(lb_logits):
    lg = lb_logits.astype(F32)
    neg_log_lb = -jax.nn.log_softmax(lg, axis=0)[0]
    return HG_CHUNK * jnp.max(neg_log_lb) > MAX_FAST_EXPONENT


def _ret_constants():
    lg = np.asarray(RET_LOG_DECAY, np.float64)
    idx = np.arange(RET_CHUNK, dtype=np.float64)
    diff = idx[:, None] - idx[None, :]
    dmask = np.where(diff >= 0, np.exp(np.where(diff >= 0, diff, 0.0)[None] * lg[:, None, None]), 0.0)
    zeta = np.exp((RET_CHUNK - 1 - idx)[None, :] * lg[:, None])
    xi = np.exp((idx + 1)[None, :] * lg[:, None])
    bc = lambda a: np.broadcast_to(a[:, :, None], (RET_HEADS, RET_CHUNK, RET_DK))
    return jnp.asarray(dmask, F32), jnp.asarray(bc(zeta), F32), jnp.asarray(bc(xi), F32)


def _chunk_tril(tm):
    i = np.arange(tm)
    m = (i[:, None] >= i[None, :]) & (i[:, None] // HG_CHUNK == i[None, :] // HG_CHUNK)
    return jnp.asarray(m, BF16)


def _fused_call(x2d, pos3, mod3, w_in_slabs, w_out_slabs, ln_w, ln_b, lb_logits, hg_norm_w, *, seq, safe):
    m, d = x2d.shape
    tm = min(TOKEN_TILE, seq)
    steps_per_batch = seq // tm
    n_tiles = m // tm
    dmask, zeta, xi = _ret_constants()
    tribd = _chunk_tril(tm)
    half = RET_DK // 2
    freqs = ROPE_BASE ** (-jnp.arange(half, dtype=F32) / half)
    freq2 = jnp.concatenate([freqs, freqs]).reshape(1, RET_DK)
    sign = jnp.concatenate([-jnp.ones((half,), F32), jnp.ones((half,), F32)]).reshape(1, RET_DK)

    last = n_tiles - 1
    cur = lambda s: jnp.minimum(s, last)
    mixd = lambda s: jnp.clip(s - 1, 0, last)
    done = lambda s: jnp.maximum(s - 2, 0)
    resident = lambda a: pl.BlockSpec(a.shape, lambda s: (0,) * a.ndim, pipeline_mode=pl.Buffered(1))
    return pl.pallas_call(
        functools.partial(_fused_kernel, tm=tm, steps_per_batch=steps_per_batch, n_tiles=n_tiles, safe=safe),
        grid=(n_tiles + 2,),
        in_specs=[pl.BlockSpec(memory_space=pl.ANY),
                  pl.BlockSpec((1, tm, 1), lambda s: (mixd(s) // steps_per_batch, mixd(s) % steps_per_batch, 0)),
                  pl.BlockSpec((1, 1, d), lambda s: (cur(s) // steps_per_batch, 0, 0)),
                  pl.BlockSpec((1, 1, d), lambda s: (cur(s) // steps_per_batch, 0, 1)),
                  pl.BlockSpec((1, 1, d), lambda s: (done(s) // steps_per_batch, 0, 2)),
                  resident(w_in_slabs), resident(w_out_slabs), resident(ln_w), resident(ln_b),
                  resident(lb_logits), resident(hg_norm_w), resident(dmask), resident(zeta),
                  resident(xi), resident(tribd), resident(freq2), resident(sign)],
        out_specs=pl.BlockSpec((tm, d), lambda s: (done(s), 0), pipeline_mode=pl.Buffered(1)),
        out_shape=jax.ShapeDtypeStruct((m, d), F32),
        scratch_shapes=[pltpu.VMEM((RET_HEADS, tm, RET_GROUP), F32),
                        pltpu.VMEM((HG_HEADS, tm, HG_GROUP), F32),
                        pltpu.VMEM((RET_HEADS, RET_DK, RET_DV), F32),
                        pltpu.VMEM((HG_HEADS, HG_DV, HG_DK), F32),
                        pltpu.VMEM((tm, MIX_WIDTH), BF16),
                        pltpu.VMEM((tm, d), F32),
                        pltpu.VMEM((tm, d), F32),
                        pltpu.VMEM((tm, d), BF16),
                        pltpu.SemaphoreType.DMA(()),
                        pltpu.SemaphoreType.DMA(())],
        compiler_params=pltpu.CompilerParams(dimension_semantics=("arbitrary",),
                                             vmem_limit_bytes=VMEM_LIMIT_BYTES),
        name="fused_layer_safe" if safe else "fused_layer",
    )(x2d, pos3, mod3, mod3, mod3, w_in_slabs, w_out_slabs, ln_w, ln_b,
      lb_logits, hg_norm_w, dmask, zeta, xi, tribd, freq2, sign)


def kernel(x, c, positions, w_ada, b_ada, w_in, lb_logits, hg_norm_w, w_out, ln_w, ln_b):
    bsz, seq, d = x.shape
    assert w_ada.shape[0] == DEPTH and seq % RET_CHUNK == 0 and d % PIECE == 0
    mod = _ada_call(c, w_ada[0], b_ada[0])
    mod3 = mod.reshape(bsz, 1, 3 * d)
    operands = (x.reshape(bsz * seq, d), positions.reshape(bsz, seq, 1), mod3, w_in[0], w_out[0],
                ln_w[0].reshape(1, d), ln_b[0].reshape(1, d),
                lb_logits.astype(F32), hg_norm_w[0].reshape(1, HG_WIDTH))

    def run(safe, x2d, pos3, mod3_, w_in_f32, w_out_f32, *rest):
        return _fused_call(x2d, pos3, mod3_, _column_slabs(w_in_f32), _column_slabs(w_out_f32),
                           *rest, seq=seq, safe=safe)

    out = lax.cond(_needs_safe_path(lb_logits),
                   lambda ops: run(True, *ops), lambda ops: run(False, *ops), operands)
    return out.reshape(bsz, seq, d)
```

```python
import functools

import numpy as np
import jax
import jax.numpy as jnp
from jax import lax
from jax.experimental import pallas as pl
from jax.experimental.pallas import tpu as pltpu

RET_HEADS = 4
RET_DK = 128
RET_DV = 256
HG_HEADS = 8
HG_DK = 128
HG_DV = 128
RET_CHUNK = 128
HG_CHUNK = 64
ROPE_BASE = 10000.0
EPS = 1e-6
DEPTH = 1
ALPHA = (2.0 * DEPTH) ** 0.25
RET_LOG2_DECAY = tuple(5.0 + 7.0 * h / (RET_HEADS - 1) for h in range(RET_HEADS))
RET_LOG_DECAY = tuple(float(np.log1p(-np.exp2(-e))) for e in RET_LOG2_DECAY)

RET_QK = RET_HEADS * RET_DK
RET_WIDTH = RET_HEADS * RET_DV
HG_QK = HG_HEADS * HG_DK
HG_WIDTH = HG_HEADS * HG_DV
MIX_WIDTH = RET_WIDTH + HG_WIDTH
OFF_RQ = 0
OFF_RK = OFF_RQ + RET_QK
OFF_RV = OFF_RK + RET_QK
OFF_RG = OFF_RV + RET_WIDTH
OFF_HQ = OFF_RG + RET_WIDTH
OFF_HF = OFF_HQ + HG_QK
OFF_HI = OFF_HF + HG_QK
OFF_HG = OFF_HI + HG_WIDTH
IN_WIDTH = OFF_HG + HG_WIDTH
RET_GROUP = 2 * RET_DK + 2 * RET_DV
HG_GROUP = 2 * HG_DK + 2 * HG_DV

F32 = jnp.float32
BF16 = jnp.bfloat16
TOKEN_TILE = 256
PIECE = 256
V7X_VMEM_BYTES = 64 * 1024 * 1024
VMEM_LIMIT_BYTES = V7X_VMEM_BYTES - 512 * 1024
HEAD_ORDER = (("ret", 0), ("ret", 1), ("hg", 0), ("hg", 1), ("hg", 2), ("hg", 3),
              ("ret", 2), ("ret", 3), ("hg", 4), ("hg", 5), ("hg", 6), ("hg", 7))
MAX_FAST_EXPONENT = 80.0
LN_ROWS = 16
LN_BLOCKS_PER_PIECE = 8


def _slab_kernel(w_ref, o_ref):
    for j in range(o_ref.shape[0]):
        o_ref[j] = w_ref[:, pl.ds(j * PIECE, PIECE)].astype(BF16)


def _column_slabs(w):
    k, n = w.shape
    per_step = 2 if (n // PIECE) % 2 == 0 else 1
    return pl.pallas_call(
        _slab_kernel,
        grid=(n // (PIECE * per_step),),
        in_specs=[pl.BlockSpec((k, PIECE * per_step), lambda j: (0, j))],
        out_specs=pl.BlockSpec((per_step, k, PIECE), lambda j: (j, 0, 0)),
        out_shape=jax.ShapeDtypeStruct((n // PIECE, k, PIECE), BF16),
        compiler_params=pltpu.CompilerParams(dimension_semantics=("arbitrary",)),
        name="weight_slabs",
    )(w)


def _sigmoid(x):
    return 1.0 / (1.0 + jnp.exp(-x))


def _ada_kernel(c_ref, w_ref, b_ref, o_ref):
    c = c_ref[...]
    cond = c * _sigmoid(c)
    o_ref[...] = jnp.dot(cond, w_ref[...], preferred_element_type=F32) + b_ref[...]


def _ada_call(c, w, b):
    bsz, d = c.shape
    n = w.shape[1]
    tn = 512 if n % 512 == 0 else n
    return pl.pallas_call(
        _ada_kernel,
        grid=(n // tn,),
        in_specs=[pl.BlockSpec((bsz, d), lambda j: (0, 0)),
                  pl.BlockSpec((d, tn), lambda j: (0, j)),
                  pl.BlockSpec((1, tn), lambda j: (0, j))],
        out_specs=pl.BlockSpec((bsz, tn), lambda j: (0, j)),
        out_shape=jax.ShapeDtypeStruct((bsz, n), F32),
        compiler_params=pltpu.CompilerParams(dimension_semantics=("arbitrary",)),
        name="ada_mod",
    )(c, w, b.reshape(1, n))


def _dot(a, b):
    return jnp.dot(a, b, preferred_element_type=F32)


def _dot_nt(a, b):
    return lax.dot_general(a, b, (((1,), (1,)), ((), ())), preferred_element_type=F32)


def _dot_tn(a, b):
    return lax.dot_general(a, b, (((0,), (0,)), ((), ())), preferred_element_type=F32)


def _split3(x):
    hi = x.astype(BF16)
    r1 = x - hi.astype(F32)
    mid = r1.astype(BF16)
    lo = (r1 - mid.astype(F32)).astype(BF16)
    return hi, mid, lo


def _layernorm(x):
    mu = jnp.mean(x, axis=-1, keepdims=True)
    xc = x - mu
    var = jnp.mean(xc * xc, axis=-1, keepdims=True)
    return xc * lax.rsqrt(var + EPS)


def _rmsnorm(x):
    return x * lax.rsqrt(jnp.mean(x * x, axis=-1, keepdims=True) + EPS)


def _fused_kernel(xhbm_ref, pos_ref, shift_ref, scale_ref, gate_ref, win_ref, wout_ref, lnw_ref,
                  lnb_ref, lbl_ref, hgw_ref, dmask_ref, zeta_ref, xi_ref, tribd_ref, freq_ref, sign_ref,
                  o_ref, pret_ref, phg_ref, sret_ref, shg_ref, mixed_ref, xin_ref, xres_ref, h_ref,
                  hold_ref, in_sem, res_sem, *, tm, steps_per_batch, n_tiles, safe):
    s = pl.program_id(0)
    n_ret_chunks = tm // RET_CHUNK
    n_hg_chunks = tm // HG_CHUNK

    def residual_copy(step):
        tile = jnp.maximum(step - 2, 0)
        return pltpu.make_async_copy(xhbm_ref.at[pl.ds(tile * tm, tm), :], xres_ref, res_sem)

    def input_copy(step):
        tile = jnp.minimum(step, n_tiles - 1)
        return pltpu.make_async_copy(xhbm_ref.at[pl.ds(tile * tm, tm), :], xin_ref, in_sem)

    @pl.when(s == 0)
    def _():
        pret_ref[...] = jnp.zeros_like(pret_ref)
        phg_ref[...] = jnp.zeros_like(phg_ref)
        mixed_ref[...] = jnp.zeros_like(mixed_ref)
        hold_ref[...] = jnp.zeros_like(hold_ref)
        residual_copy(s).start()
        input_copy(s).start()

    @pl.when(lax.rem(jnp.maximum(s - 1, 0), steps_per_batch) == 0)
    def _():
        sret_ref[...] = jnp.zeros_like(sret_ref)
        shg_ref[...] = jnp.zeros_like(shg_ref)

    residual_copy(s).wait()
    input_copy(s).wait()

    prev_q = []
    pending = []

    def fill(n=1):
        for _ in range(n):
            while prev_q or pending:
                is_matmul, thunk = (prev_q or pending).pop(0)
                thunk()
                if is_matmul:
                    break

    def finish_previous_tile():
        while prev_q:
            fill()

    mixed_heads = set()
    deferred_pair = None if safe else (HEAD_ORDER[-1][0], HEAD_ORDER[-1][1] & ~1)

    def release(kind, h):
        mixed_heads.add((kind, h))
        if (kind, h ^ 1) in mixed_heads and (kind, h & ~1) != deferred_pair:
            pending.extend((True, p) for p in proj_pieces(kind, h & ~1, h_ref))

    def proj_pieces(kind, h0, lhs_ref):
        dst_ref = pret_ref if kind == "ret" else phg_ref

        def piece(src_col, dests):
            def run():
                r = _dot(lhs_ref[...], win_ref[src_col // PIECE])
                c = 0
                for head, col, width in dests:
                    dst_ref[head, :, pl.ds(col, width)] = r[:, c:c + width]
                    c += width
            return run

        if kind == "ret":
            pieces = [piece(off + h0 * RET_DK, [(h0, col, RET_DK), (h0 + 1, col, RET_DK)])
                      for off, col in ((OFF_RQ, 0), (OFF_RK, RET_DK))]
            pieces += [piece(off + h * RET_DV, [(h, col, RET_DV)])
                       for off, col in ((OFF_RV, 2 * RET_DK), (OFF_RG, 2 * RET_DK + RET_DV))
                       for h in (h0, h0 + 1)]
            return pieces
        return [piece(off + h0 * HG_DK, [(h0, col, HG_DK), (h0 + 1, col, HG_DK)])
                for off, col in ((OFF_HQ, 0), (OFF_HF, HG_DK), (OFF_HI, 2 * HG_DK), (OFF_HG, 2 * HG_DK + HG_DV))]

    def outproj_piece(c0):
        def run():
            o_ref[:, pl.ds(c0, PIECE)] = _dot(mixed_ref[...], wout_ref[c0 // PIECE])
        return run

    def final_layernorm():
        for r0 in range(0, tm, LN_ROWS):
            rows = pl.ds(r0, LN_ROWS)
            z = ALPHA * xres_ref[rows, :] + gate_ref[0] * o_ref[rows, :]
            o_ref[rows, :] = _layernorm(z) * lnw_ref[...] + lnb_ref[...]

        @pl.when(s + 1 < pl.num_programs(0))
        def _():
            residual_copy(s + 1).start()

    prev_q.extend((True, outproj_piece(c0)) for c0 in range(0, o_ref.shape[1], PIECE))
    prev_q.append((False, final_layernorm))
    if deferred_pair is not None:
        pending.extend((True, p) for p in proj_pieces(*deferred_pair, hold_ref))

    for i, r0 in enumerate(range(0, tm, LN_ROWS)):
        rows = pl.ds(r0, LN_ROWS)
        h_ref[rows, :] = (_layernorm(xin_ref[rows, :]) * (1.0 + scale_ref[0]) + shift_ref[0]).astype(BF16)
        if i % LN_BLOCKS_PER_PIECE == LN_BLOCKS_PER_PIECE - 1:
            fill()

    @pl.when(s + 1 < pl.num_programs(0))
    def _():
        input_copy(s + 1).start()

    ang = pos_ref[0].astype(F32) * freq_ref[...]
    cosf = jnp.cos(ang)
    sinf = jnp.sin(ang) * sign_ref[...]
    fill()

    def mix_ret(h):
        cdec = float(np.exp(RET_CHUNK * RET_LOG_DECAY[h]))
        scores, kv, qx, vs = [], [], [], []
        for c in range(n_ret_chunks):
            rows = pl.ds(c * RET_CHUNK, RET_CHUNK)
            cf = cosf[c * RET_CHUNK:(c + 1) * RET_CHUNK]
            sf = sinf[c * RET_CHUNK:(c + 1) * RET_CHUNK]
            q = pret_ref[h, rows, pl.ds(0, RET_DK)]
            k = pret_ref[h, rows, pl.ds(RET_DK, RET_DK)]
            v = pret_ref[h, rows, pl.ds(2 * RET_DK, RET_DV)].astype(BF16)
            q = (q * cf + pltpu.roll(q, RET_DK // 2, 1) * sf) * (RET_DK ** -0.5)
            k = k * cf + pltpu.roll(k, RET_DK // 2, 1) * sf
            scores.append(_dot_nt(q.astype(BF16), k.astype(BF16)))
            kv.append(_dot_tn((k * zeta_ref[h]).astype(BF16), v))
            qx.append((q * xi_ref[h]).astype(BF16))
            vs.append(v)
        fill()
        state = sret_ref[h]
        rets = []
        for c in range(n_ret_chunks):
            p = (scores[c] * dmask_ref[h]).astype(BF16)
            rets.append(_dot(p, vs[c]) + _dot(qx[c], state.astype(BF16)))
            state = cdec * state + kv[c]
        sret_ref[h] = state
        fill()
        finish_previous_tile()
        for c in range(n_ret_chunks):
            rows = pl.ds(c * RET_CHUNK, RET_CHUNK)
            g = pret_ref[h, rows, pl.ds(2 * RET_DK + RET_DV, RET_DV)]
            mixed_ref[rows, pl.ds(h * RET_DV, RET_DV)] = (_rmsnorm(rets[c]) * (g * _sigmoid(g))).astype(BF16)
        fill()
        release("ret", h)

    l0 = lbl_ref[0:1, :]
    l1 = lbl_ref[1:2, :]
    lmax = jnp.maximum(l0, l1)
    e0 = jnp.exp(l0 - lmax)
    lb_all = e0 / (e0 + jnp.exp(l1 - lmax))
    row = lax.broadcasted_iota(jnp.int32, (HG_CHUNK, HG_CHUNK), 0)
    col = lax.broadcasted_iota(jnp.int32, (HG_CHUNK, HG_CHUNK), 1)
    causal = row >= col

    def intra_chunk_scores_safe(q, k, bc):
        trow = lax.broadcasted_iota(jnp.int32, (HG_CHUNK, HG_DK), 0)
        xor = jnp.bitwise_xor(row, col)
        acc = jnp.where(row == col, jnp.sum(q * k, axis=-1, keepdims=True), 0.0)
        half = 1
        while half < HG_CHUNK:
            block = 2 * half
            upper = jnp.bitwise_and(trow, half) != 0
            if block >= 8:
                b3 = bc.reshape(HG_CHUNK // block, block, HG_DK)
                bm = jnp.broadcast_to(b3[:, half - 1:half, :], b3.shape).reshape(HG_CHUNK, HG_DK)
            elif half == 2:
                b3 = bc.reshape(HG_CHUNK // 8, 8, HG_DK)
                sub = lax.broadcasted_iota(jnp.int32, b3.shape, 1)
                bm = jnp.where(sub < 4, jnp.broadcast_to(b3[:, 1:2, :], b3.shape),
                               jnp.broadcast_to(b3[:, 5:6, :], b3.shape)).reshape(HG_CHUNK, HG_DK)
            else:
                bm = jnp.where(upper, pltpu.roll(bc, 1, 0), bc)
            u = (jnp.where(upper, q, k) * jnp.exp(-jnp.abs(bc - bm))).astype(BF16)
            level = (row > col) & (xor >= half) & (xor < block)
            acc = jnp.where(level, _dot_nt(u, u), acc)
            half = block
        return acc

    def mix_hg(h):
        lb = lb_all[:, h * HG_DK:(h + 1) * HG_DK]
        fp = phg_ref[h, :, pl.ds(HG_DK, HG_DK)]
        sg = _sigmoid(fp)
        logf = jnp.log(lb + (1.0 - lb) * sg)
        kk = (1.0 - lb) * (1.0 - sg)
        parts = _dot(tribd_ref[...], jnp.concatenate(_split3(logf), axis=1))
        b = parts[:, 0:HG_DK] + parts[:, HG_DK:2 * HG_DK] + parts[:, 2 * HG_DK:3 * HG_DK]
        fill()
        hq = phg_ref[h, :, pl.ds(0, HG_DK)]
        qd = (hq * jnp.exp(b)).astype(BF16)
        if not safe:
            kd = (kk * jnp.exp(-b)).astype(BF16)
        attn, kvt, elast, vs = [], [], [], []
        for c in range(n_hg_chunks):
            r0, r1 = c * HG_CHUNK, (c + 1) * HG_CHUNK
            b_last = b[r1 - 1:r1, :]
            ks = (kk[r0:r1] * jnp.exp(b_last - b[r0:r1])).astype(BF16)
            v = phg_ref[h, pl.ds(r0, HG_CHUNK), pl.ds(2 * HG_DK, HG_DV)].astype(BF16)
            if safe:
                attn.append(intra_chunk_scores_safe(hq[r0:r1], kk[r0:r1], b[r0:r1]))
            else:
                attn.append(jnp.where(causal, _dot_nt(qd[r0:r1], kd[r0:r1]), 0.0))
            kvt.append(_dot_tn(v, ks))
            elast.append(jnp.exp(b_last))
            vs.append(v)
        fill()
        state_t = shg_ref[h]
        outs = []
        for c in range(n_hg_chunks):
            r0, r1 = c * HG_CHUNK, (c + 1) * HG_CHUNK
            outs.append(_dot(attn[c].astype(BF16), vs[c]) + _dot_nt(qd[r0:r1], state_t.astype(BF16)))
            state_t = state_t * elast[c] + kvt[c]
        shg_ref[h] = state_t
        finish_previous_tile()
        hgw = hgw_ref[:, pl.ds(h * HG_DV, HG_DV)]
        for c in range(n_hg_chunks):
            rows = pl.ds(c * HG_CHUNK, HG_CHUNK)
            g = phg_ref[h, rows, pl.ds(2 * HG_DK + HG_DV, HG_DV)]
            mixed_ref[rows, pl.ds(RET_WIDTH + h * HG_DV, HG_DV)] = (
                _rmsnorm(outs[c]) * hgw * (g * _sigmoid(g))).astype(BF16)
        release("hg", h)

    for kind, h in HEAD_ORDER:
        if kind == "ret":
            mix_ret(h)
        else:
            mix_hg(h)
    fill(len(pending))
    if deferred_pair is not None:
        hold_ref[...] = h_ref[...]


def _needs_safe_path(lb_logits):
    lg = lb_logits.astype(F32)
    neg_log_lb = -jax.nn.log_softmax(lg, axis=0)[0]
    return HG_CHUNK * jnp.max(neg_log_lb) > MAX_FAST_EXPONENT


def _ret_constants():
    lg = np.asarray(RET_LOG_DECAY, np.float64)
    idx = np.arange(RET_CHUNK, dtype=np.float64)
    diff = idx[:, None] - idx[None, :]
    dmask = np.where(diff >= 0, np.exp(np.where(diff >= 0, diff, 0.0)[None] * lg[:, None, None]), 0.0)
    zeta = np.exp((RET_CHUNK - 1 - idx)[None, :] * lg[:, None])
    xi = np.exp((idx + 1)[None, :] * lg[:, None])
    bc = lambda a: np.broadcast_to(a[:, :, None], (RET_HEADS, RET_CHUNK, RET_DK))
    return jnp.asarray(dmask, F32), jnp.asarray(bc(zeta), F32), jnp.asarray(bc(xi), F32)


def _chunk_tril(tm):
    i = np.arange(tm)
    m = (i[:, None] >= i[None, :]) & (i[:, None] // HG_CHUNK == i[None, :] // HG_CHUNK)
    return jnp.asarray(m, BF16)


def _fused_call(x2d, pos3, mod3, w_in_slabs, w_out_slabs, ln_w, ln_b, lb_logits, hg_norm_w, *, seq, safe):
    m, d = x2d.shape
    tm = min(TOKEN_TILE, seq)
    steps_per_batch = seq // tm
    n_tiles = m // tm
    dmask, zeta, xi = _ret_constants()
    tribd = _chunk_tril(tm)
    half = RET_DK // 2
    freqs = ROPE_BASE ** (-jnp.arange(half, dtype=F32) / half)
    freq2 = jnp.concatenate([freqs, freqs]).reshape(1, RET_DK)
    sign = jnp.concatenate([-jnp.ones((half,), F32), jnp.ones((half,), F32)]).reshape(1, RET_DK)

    last = n_tiles - 1
    cur = lambda s: jnp.minimum(s, last)
    mixd = lambda s: jnp.clip(s - 1, 0, last)
    done = lambda s: jnp.maximum(s - 2, 0)
    resident = lambda a: pl.BlockSpec(a.shape, lambda s: (0,) * a.ndim, pipeline_mode=pl.Buffered(1))
    return pl.pallas_call(
        functools.partial(_fused_kernel, tm=tm, steps_per_batch=steps_per_batch, n_tiles=n_tiles, safe=safe),
        grid=(n_tiles + 2,),
        in_specs=[pl.BlockSpec(memory_space=pl.ANY),
                  pl.BlockSpec((1, tm, 1), lambda s: (mixd(s) // steps_per_batch, mixd(s) % steps_per_batch, 0)),
                  pl.BlockSpec((1, 1, d), lambda s: (cur(s) // steps_per_batch, 0, 0)),
                  pl.BlockSpec((1, 1, d), lambda s: (cur(s) // steps_per_batch, 0, 1)),
                  pl.BlockSpec((1, 1, d), lambda s: (done(s) // steps_per_batch, 0, 2)),
                  resident(w_in_slabs), resident(w_out_slabs), resident(ln_w), resident(ln_b),
                  resident(lb_logits), resident(hg_norm_w), resident(dmask), resident(zeta),
                  resident(xi), resident(tribd), resident(freq2), resident(sign)],
        out_specs=pl.BlockSpec((tm, d), lambda s: (done(s), 0), pipeline_mode=pl.Buffered(1)),
        out_shape=jax.ShapeDtypeStruct((m, d), F32),
        scratch_shapes=[pltpu.VMEM((RET_HEADS, tm, RET_GROUP), F32),
                        pltpu.VMEM((HG_HEADS, tm, HG_GROUP), F32),
                        pltpu.VMEM((RET_HEADS, RET_DK, RET_DV), F32),
                        pltpu.VMEM((HG_HEADS, HG_DV, HG_DK), F32),
                        pltpu.VMEM((tm, MIX_WIDTH), BF16),
                        pltpu.VMEM((tm, d), F32),
                        pltpu.VMEM((tm, d), F32),
                        pltpu.VMEM((tm, d), BF16),
                        pltpu.VMEM((16, 128) if safe else (tm, d), BF16),
                        pltpu.SemaphoreType.DMA(()),
                        pltpu.SemaphoreType.DMA(())],
        compiler_params=pltpu.CompilerParams(dimension_semantics=("arbitrary",),
                                             vmem_limit_bytes=VMEM_LIMIT_BYTES),
        name="fused_layer_safe" if safe else "fused_layer",
    )(x2d, pos3, mod3, mod3, mod3, w_in_slabs, w_out_slabs, ln_w, ln_b,
      lb_logits, hg_norm_w, dmask, zeta, xi, tribd, freq2, sign)


def kernel(x, c, positions, w_ada, b_ada, w_in, lb_logits, hg_norm_w, w_out, ln_w, ln_b):
    bsz, seq, d = x.shape
    assert w_ada.shape[0] == DEPTH and seq % RET_CHUNK == 0 and d % PIECE == 0
    mod = _ada_call(c, w_ada[0], b_ada[0])
    mod3 = mod.reshape(bsz, 1, 3 * d)
    operands = (x.reshape(bsz * seq, d), positions.reshape(bsz, seq, 1), mod3, w_in[0], w_out[0],
                ln_w[0].reshape(1, d), ln_b[0].reshape(1, d),
                lb_logits.astype(F32), hg_norm_w[0].reshape(1, HG_WIDTH))

    def run(safe, x2d, pos3, mod3_, w_in_f32, w_out_f32, *rest):
        return _fused_call(x2d, pos3, mod3_, _column_slabs(w_in_f32), _column_slabs(w_out_f32),
                           *rest, seq=seq, safe=safe)

    out = lax.cond(_needs_safe_path(lb_logits),
                   lambda ops: run(True, *ops), lambda ops: run(False, *ops), operands)
    return out.reshape(bsz, seq, d)
```

```python
import functools

import numpy as np
import jax
import jax.numpy as jnp
from jax import lax
from jax.experimental import pallas as pl
from jax.experimental.pallas import tpu as pltpu

RET_HEADS = 4
RET_DK = 128
RET_DV = 256
HG_HEADS = 8
HG_DK = 128
HG_DV = 128
RET_CHUNK = 128
HG_CHUNK = 64
ROPE_BASE = 10000.0
EPS = 1e-6
DEPTH = 1
ALPHA = (2.0 * DEPTH) ** 0.25
RET_LOG2_DECAY = tuple(5.0 + 7.0 * h / (RET_HEADS - 1) for h in range(RET_HEADS))
RET_LOG_DECAY = tuple(float(np.log1p(-np.exp2(-e))) for e in RET_LOG2_DECAY)

RET_QK = RET_HEADS * RET_DK
RET_WIDTH = RET_HEADS * RET_DV
HG_QK = HG_HEADS * HG_DK
HG_WIDTH = HG_HEADS * HG_DV
MIX_WIDTH = RET_WIDTH + HG_WIDTH
OFF_RQ = 0
OFF_RK = OFF_RQ + RET_QK
OFF_RV = OFF_RK + RET_QK
OFF_RG = OFF_RV + RET_WIDTH
OFF_HQ = OFF_RG + RET_WIDTH
OFF_HF = OFF_HQ + HG_QK
OFF_HI = OFF_HF + HG_QK
OFF_HG = OFF_HI + HG_WIDTH
IN_WIDTH = OFF_HG + HG_WIDTH
RET_GROUP = 2 * RET_DK + 2 * RET_DV
HG_GROUP = 2 * HG_DK + 2 * HG_DV

F32 = jnp.float32
BF16 = jnp.bfloat16
TOKEN_TILE = 256
PIECE = 256
V7X_VMEM_BYTES = 64 * 1024 * 1024
VMEM_LIMIT_BYTES = V7X_VMEM_BYTES - 512 * 1024
HEAD_ORDER = (("ret", 0), ("ret", 1), ("hg", 0), ("hg", 1), ("hg", 2), ("hg", 3),
              ("ret", 2), ("ret", 3), ("hg", 4), ("hg", 5), ("hg", 6), ("hg", 7))
MAX_FAST_EXPONENT = 80.0
LN_ROWS = 16
LN_BLOCKS_PER_PIECE = 8


def _slab_kernel(w_ref, o_ref):
    for j in range(o_ref.shape[0]):
        o_ref[j] = w_ref[:, pl.ds(j * PIECE, PIECE)].astype(BF16)


def _column_slabs(w):
    k, n = w.shape
    per_step = 2 if (n // PIECE) % 2 == 0 else 1
    return pl.pallas_call(
        _slab_kernel,
        grid=(n // (PIECE * per_step),),
        in_specs=[pl.BlockSpec((k, PIECE * per_step), lambda j: (0, j))],
        out_specs=pl.BlockSpec((per_step, k, PIECE), lambda j: (j, 0, 0)),
        out_shape=jax.ShapeDtypeStruct((n // PIECE, k, PIECE), BF16),
        compiler_params=pltpu.CompilerParams(dimension_semantics=("arbitrary",)),
        name="weight_slabs",
    )(w)


def _sigmoid(x):
    return 1.0 / (1.0 + jnp.exp(-x))


def _ada_kernel(c_ref, w_ref, b_ref, o_ref):
    c = c_ref[...]
    cond = c * _sigmoid(c)
    o_ref[...] = jnp.dot(cond, w_ref[...], preferred_element_type=F32) + b_ref[...]


def _ada_call(c, w, b):
    bsz, d = c.shape
    n = w.shape[1]
    tn = 512 if n % 512 == 0 else n
    return pl.pallas_call(
        _ada_kernel,
        grid=(n // tn,),
        in_specs=[pl.BlockSpec((bsz, d), lambda j: (0, 0)),
                  pl.BlockSpec((d, tn), lambda j: (0, j)),
                  pl.BlockSpec((1, tn), lambda j: (0, j))],
        out_specs=pl.BlockSpec((bsz, tn), lambda j: (0, j)),
        out_shape=jax.ShapeDtypeStruct((bsz, n), F32),
        compiler_params=pltpu.CompilerParams(dimension_semantics=("arbitrary",)),
        name="ada_mod",
    )(c, w, b.reshape(1, n))


def _dot(a, b):
    return jnp.dot(a, b, preferred_element_type=F32)


def _dot_nt(a, b):
    return lax.dot_general(a, b, (((1,), (1,)), ((), ())), preferred_element_type=F32)


def _dot_tn(a, b):
    return lax.dot_general(a, b, (((0,), (0,)), ((), ())), preferred_element_type=F32)


def _split3(x):
    hi = x.astype(BF16)
    r1 = x - hi.astype(F32)
    mid = r1.astype(BF16)
    lo = (r1 - mid.astype(F32)).astype(BF16)
    return hi, mid, lo


def _layernorm(x):
    mu = jnp.mean(x, axis=-1, keepdims=True)
    xc = x - mu
    var = jnp.mean(xc * xc, axis=-1, keepdims=True)
    return xc * lax.rsqrt(var + EPS)


def _rmsnorm(x):
    return x * lax.rsqrt(jnp.mean(x * x, axis=-1, keepdims=True) + EPS)


def _fused_kernel(xhbm_ref, pos_ref, shift_ref, scale_ref, gate_ref, win_ref, wout_ref, lnw_ref,
                  lnb_ref, lbl_ref, hgw_ref, dmask_ref, zeta_ref, xi_ref, tribd_ref, freq_ref, sign_ref,
                  o_ref, pret_ref, phg_ref, sret_ref, shg_ref, mixed_ref, xin_ref, xres_ref, h_ref,
                  hold_ref, in_sem, res_sem, *, tm, steps_per_batch, n_tiles, safe):
    s = pl.program_id(0)
    n_ret_chunks = tm // RET_CHUNK
    n_hg_chunks = tm // HG_CHUNK

    def residual_copy(step):
        tile = jnp.maximum(step - 2, 0)
        return pltpu.make_async_copy(xhbm_ref.at[pl.ds(tile * tm, tm), :], xres_ref, res_sem)

    def input_copy(step):
        tile = jnp.minimum(step, n_tiles - 1)
        return pltpu.make_async_copy(xhbm_ref.at[pl.ds(tile * tm, tm), :], xin_ref, in_sem)

    @pl.when(s == 0)
    def _():
        pret_ref[...] = jnp.zeros_like(pret_ref)
        phg_ref[...] = jnp.zeros_like(phg_ref)
        mixed_ref[...] = jnp.zeros_like(mixed_ref)
        hold_ref[...] = jnp.zeros_like(hold_ref)
        residual_copy(s).start()
        input_copy(s).start()

    @pl.when(lax.rem(jnp.maximum(s - 1, 0), steps_per_batch) == 0)
    def _():
        sret_ref[...] = jnp.zeros_like(sret_ref)
        shg_ref[...] = jnp.zeros_like(shg_ref)

    residual_copy(s).wait()
    input_copy(s).wait()

    prev_q = []
    pending = []

    def fill(n=1):
        for _ in range(n):
            while prev_q or pending:
                is_matmul, thunk = (prev_q or pending).pop(0)
                thunk()
                if is_matmul:
                    break

    def finish_previous_tile():
        while prev_q:
            fill()

    mixed_heads = set()
    deferred_pair = None if safe else (HEAD_ORDER[-1][0], HEAD_ORDER[-1][1] & ~1)

    def release(kind, h):
        mixed_heads.add((kind, h))
        if (kind, h ^ 1) in mixed_heads and (kind, h & ~1) != deferred_pair:
            pending.extend((True, p) for p in proj_pieces(kind, h & ~1, h_ref))

    def proj_pieces(kind, h0, lhs_ref):
        dst_ref = pret_ref if kind == "ret" else phg_ref

        def piece(src_col, dests):
            def run():
                r = _dot(lhs_ref[...], win_ref[src_col // PIECE])
                c = 0
                for head, col, width in dests:
                    dst_ref[head, :, pl.ds(col, width)] = r[:, c:c + width]
                    c += width
            return run

        if kind == "ret":
            pieces = [piece(off + h0 * RET_DK, [(h0, col, RET_DK), (h0 + 1, col, RET_DK)])
                      for off, col in ((OFF_RQ, 0), (OFF_RK, RET_DK))]
            pieces += [piece(off + h * RET_DV, [(h, col, RET_DV)])
                       for off, col in ((OFF_RV, 2 * RET_DK), (OFF_RG, 2 * RET_DK + RET_DV))
                       for h in (h0, h0 + 1)]
            return pieces
        return [piece(off + h0 * HG_DK, [(h0, col, HG_DK), (h0 + 1, col, HG_DK)])
                for off, col in ((OFF_HQ, 0), (OFF_HF, HG_DK), (OFF_HI, 2 * HG_DK), (OFF_HG, 2 * HG_DK + HG_DV))]

    def outproj_piece(c0):
        def run():
            o_ref[:, pl.ds(c0, PIECE)] = _dot(mixed_ref[...], wout_ref[c0 // PIECE])
        return run

    def final_layernorm():
        for r0 in range(0, tm, LN_ROWS):
            rows = pl.ds(r0, LN_ROWS)
            z = ALPHA * xres_ref[rows, :] + gate_ref[0] * o_ref[rows, :]
            o_ref[rows, :] = _layernorm(z) * lnw_ref[...] + lnb_ref[...]

        @pl.when(s + 1 < pl.num_programs(0))
        def _():
            residual_copy(s + 1).start()

    prev_q.extend((True, outproj_piece(c0)) for c0 in range(0, o_ref.shape[1], PIECE))
    prev_q.append((False, final_layernorm))
    if deferred_pair is not None:
        pending.extend((True, p) for p in proj_pieces(*deferred_pair, hold_ref))

    for i, r0 in enumerate(range(0, tm, LN_ROWS)):
        rows = pl.ds(r0, LN_ROWS)
        h_ref[rows, :] = (_layernorm(xin_ref[rows, :]) * (1.0 + scale_ref[0]) + shift_ref[0]).astype(BF16)
        if i % LN_BLOCKS_PER_PIECE == LN_BLOCKS_PER_PIECE - 1:
            fill()

    @pl.when(s + 1 < pl.num_programs(0))
    def _():
        input_copy(s + 1).start()

    ang = pos_ref[0].astype(F32) * freq_ref[...]
    cosf = jnp.cos(ang)
    sinf = jnp.sin(ang) * sign_ref[...]
    fill()

    def mix_ret(h):
        cdec = float(np.exp(RET_CHUNK * RET_LOG_DECAY[h]))
        scores, kv, qx, vs = [], [], [], []
        for c in range(n_ret_chunks):
            rows = pl.ds(c * RET_CHUNK, RET_CHUNK)
            cf = cosf[c * RET_CHUNK:(c + 1) * RET_CHUNK]
            sf = sinf[c * RET_CHUNK:(c + 1) * RET_CHUNK]
            q = pret_ref[h, rows, pl.ds(0, RET_DK)]
            k = pret_ref[h, rows, pl.ds(RET_DK, RET_DK)]
            v = pret_ref[h, rows, pl.ds(2 * RET_DK, RET_DV)].astype(BF16)
            q = (q * cf + pltpu.roll(q, RET_DK // 2, 1) * sf) * (RET_DK ** -0.5)
            k = k * cf + pltpu.roll(k, RET_DK // 2, 1) * sf
            scores.append(_dot_nt(q.astype(BF16), k.astype(BF16)))
            kv.append(_dot_tn((k * zeta_ref[h]).astype(BF16), v))
            qx.append((q * xi_ref[h]).astype(BF16))
            vs.append(v)
        fill()
        state = sret_ref[h]
        rets = []
        for c in range(n_ret_chunks):
            p = (scores[c] * dmask_ref[h]).astype(BF16)
            rets.append(_dot(p, vs[c]) + _dot(qx[c], state.astype(BF16)))
            state = cdec * state + kv[c]
        sret_ref[h] = state
        fill()
        finish_previous_tile()
        for c in range(n_ret_chunks):
            rows = pl.ds(c * RET_CHUNK, RET_CHUNK)
            g = pret_ref[h, rows, pl.ds(2 * RET_DK + RET_DV, RET_DV)]
            mixed_ref[rows, pl.ds(h * RET_DV, RET_DV)] = (_rmsnorm(rets[c]) * (g * _sigmoid(g))).astype(BF16)
        fill()
        release("ret", h)

    l0 = lbl_ref[0:1, :]
    l1 = lbl_ref[1:2, :]
    lmax = jnp.maximum(l0, l1)
    e0 = jnp.exp(l0 - lmax)
    lb_all = e0 / (e0 + jnp.exp(l1 - lmax))
    row = lax.broadcasted_iota(jnp.int32, (HG_CHUNK, HG_CHUNK), 0)
    col = lax.broadcasted_iota(jnp.int32, (HG_CHUNK, HG_CHUNK), 1)
    causal = row >= col

    def intra_chunk_scores_safe(q, k, bc):
        trow = lax.broadcasted_iota(jnp.int32, (HG_CHUNK, HG_DK), 0)
        xor = jnp.bitwise_xor(row, col)
        acc = jnp.where(row == col, jnp.sum(q * k, axis=-1, keepdims=True), 0.0)
        half = 1
        while half < HG_CHUNK:
            block = 2 * half
            upper = jnp.bitwise_and(trow, half) != 0
            if block >= 8:
                b3 = bc.reshape(HG_CHUNK // block, block, HG_DK)
                bm = jnp.broadcast_to(b3[:, half - 1:half, :], b3.shape).reshape(HG_CHUNK, HG_DK)
            elif half == 2:
                b3 = bc.reshape(HG_CHUNK // 8, 8, HG_DK)
                sub = lax.broadcasted_iota(jnp.int32, b3.shape, 1)
                bm = jnp.where(sub < 4, jnp.broadcast_to(b3[:, 1:2, :], b3.shape),
                               jnp.broadcast_to(b3[:, 5:6, :], b3.shape)).reshape(HG_CHUNK, HG_DK)
            else:
                bm = jnp.where(upper, pltpu.roll(bc, 1, 0), bc)
            u = (jnp.where(upper, q, k) * jnp.exp(-jnp.abs(bc - bm))).astype(BF16)
            level = (row > col) & (xor >= half) & (xor < block)
            acc = jnp.where(level, _dot_nt(u, u), acc)
            half = block
        return acc

    def mix_hg(h):
        lb = lb_all[:, h * HG_DK:(h + 1) * HG_DK]
        fp = phg_ref[h, :, pl.ds(HG_DK, HG_DK)]
        sg = _sigmoid(fp)
        logf = jnp.log(lb + (1.0 - lb) * sg)
        kk = (1.0 - lb) * (1.0 - sg)
        parts = _dot(tribd_ref[...], jnp.concatenate(_split3(logf), axis=1))
        b = parts[:, 0:HG_DK] + parts[:, HG_DK:2 * HG_DK] + parts[:, 2 * HG_DK:3 * HG_DK]
        fill()
        hq = phg_ref[h, :, pl.ds(0, HG_DK)]
        qd = (hq * jnp.exp(b)).astype(BF16)
        if not safe:
            kd = (kk * jnp.exp(-b)).astype(BF16)
        attn, kvt, elast, vs = [], [], [], []
        for c in range(n_hg_chunks):
            r0, r1 = c * HG_CHUNK, (c + 1) * HG_CHUNK
            b_last = b[r1 - 1:r1, :]
            ks = (kk[r0:r1] * jnp.exp(b_last - b[r0:r1])).astype(BF16)
            v = phg_ref[h, pl.ds(r0, HG_CHUNK), pl.ds(2 * HG_DK, HG_DV)].astype(BF16)
            if safe:
                attn.append(intra_chunk_scores_safe(hq[r0:r1], kk[r0:r1], b[r0:r1]))
            else:
                attn.append(jnp.where(causal, _dot_nt(qd[r0:r1], kd[r0:r1]), 0.0))
            kvt.append(_dot_tn(v, ks))
            elast.append(jnp.exp(b_last))
            vs.append(v)
        fill()
        state_t = shg_ref[h]
        outs = []
        for c in range(n_hg_chunks):
            r0, r1 = c * HG_CHUNK, (c + 1) * HG_CHUNK
            outs.append(_dot(attn[c].astype(BF16), vs[c]) + _dot_nt(qd[r0:r1], state_t.astype(BF16)))
            state_t = state_t * elast[c] + kvt[c]
        shg_ref[h] = state_t
        fill()
        finish_previous_tile()
        hgw = hgw_ref[:, pl.ds(h * HG_DV, HG_DV)]
        for c in range(n_hg_chunks):
            rows = pl.ds(c * HG_CHUNK, HG_CHUNK)
            g = phg_ref[h, rows, pl.ds(2 * HG_DK + HG_DV, HG_DV)]
            mixed_ref[rows, pl.ds(RET_WIDTH + h * HG_DV, HG_DV)] = (
                _rmsnorm(outs[c]) * hgw * (g * _sigmoid(g))).astype(BF16)
        release("hg", h)

    for kind, h in HEAD_ORDER:
        if kind == "ret":
            mix_ret(h)
        else:
            mix_hg(h)
    fill(len(pending))
    if deferred_pair is not None:
        hold_ref[...] = h_ref[...]


def _needs_safe_path(lb_logits):
    lg = lb_logits.astype(F32)
    neg_log_lb = -jax.nn.log_softmax(lg, axis=0)[0]
    return HG_CHUNK * jnp.max(neg_log_lb) > MAX_FAST_EXPONENT


def _ret_constants():
    lg = np.asarray(RET_LOG_DECAY, np.float64)
    idx = np.arange(RET_CHUNK, dtype=np.float64)
    diff = idx[:, None] - idx[None, :]
    dmask = np.where(diff >= 0, np.exp(np.where(diff >= 0, diff, 0.0)[None] * lg[:, None, None]), 0.0)
    zeta = np.exp((RET_CHUNK - 1 - idx)[None, :] * lg[:, None])
    xi = np.exp((idx + 1)[None, :] * lg[:, None])
    bc = lambda a: np.broadcast_to(a[:, :, None], (RET_HEADS, RET_CHUNK, RET_DK))
    return jnp.asarray(dmask, F32), jnp.asarray(bc(zeta), F32), jnp.asarray(bc(xi), F32)


def _chunk_tril(tm):
    i = np.arange(tm)
    m = (i[:, None] >= i[None, :]) & (i[:, None] // HG_CHUNK == i[None, :] // HG_CHUNK)
    return jnp.asarray(m, BF16)


def _fused_call(x2d, pos3, mod3, w_in_slabs, w_out_slabs, ln_w, ln_b, lb_logits, hg_norm_w, *, seq, safe):
    m, d = x2d.shape
    tm = min(TOKEN_TILE, seq)
    steps_per_batch = seq // tm
    n_tiles = m // tm
    dmask, zeta, xi = _ret_constants()
    tribd = _chunk_tril(tm)
    half = RET_DK // 2
    freqs = ROPE_BASE ** (-jnp.arange(half, dtype=F32) / half)
    freq2 = jnp.concatenate([freqs, freqs]).reshape(1, RET_DK)
    sign = jnp.concatenate([-jnp.ones((half,), F32), jnp.ones((half,), F32)]).reshape(1, RET_DK)

    last = n_tiles - 1
    cur = lambda s: jnp.minimum(s, last)
    mixd = lambda s: jnp.clip(s - 1, 0, last)
    done = lambda s: jnp.maximum(s - 2, 0)
    resident = lambda a: pl.BlockSpec(a.shape, lambda s: (0,) * a.ndim, pipeline_mode=pl.Buffered(1))
    return pl.pallas_call(
        functools.partial(_fused_kernel, tm=tm, steps_per_batch=steps_per_batch, n_tiles=n_tiles, safe=safe),
        grid=(n_tiles + 2,),
        in_specs=[pl.BlockSpec(memory_space=pl.ANY),
                  pl.BlockSpec((1, tm, 1), lambda s: (mixd(s) // steps_per_batch, mixd(s) % steps_per_batch, 0)),
                  pl.BlockSpec((1, 1, d), lambda s: (cur(s) // steps_per_batch, 0, 0)),
                  pl.BlockSpec((1, 1, d), lambda s: (cur(s) // steps_per_batch, 0, 1)),
                  pl.BlockSpec((1, 1, d), lambda s: (done(s) // steps_per_batch, 0, 2)),
                  resident(w_in_slabs), resident(w_out_slabs), resident(ln_w), resident(ln_b),
                  resident(lb_logits), resident(hg_norm_w), resident(dmask), resident(zeta),
                  resident(xi), resident(tribd), resident(freq2), resident(sign)],
        out_specs=pl.BlockSpec((tm, d), lambda s: (done(s), 0), pipeline_mode=pl.Buffered(1)),
        out_shape=jax.ShapeDtypeStruct((m, d), F32),
        scratch_shapes=[pltpu.VMEM((RET_HEADS, tm, RET_GROUP), F32),
                        pltpu.VMEM((HG_HEADS, tm, HG_GROUP), F32),
                        pltpu.VMEM((RET_HEADS, RET_DK, RET_DV), F32),
                        pltpu.VMEM((HG_HEADS, HG_DV, HG_DK), F32),
                        pltpu.VMEM((tm, MIX_WIDTH), BF16),
                        pltpu.VMEM((tm, d), F32),
                        pltpu.VMEM((tm, d), F32),
                        pltpu.VMEM((tm, d), BF16),
                        pltpu.VMEM((16, 128) if safe else (tm, d), BF16),
                        pltpu.SemaphoreType.DMA(()),
                        pltpu.SemaphoreType.DMA(())],
        compiler_params=pltpu.CompilerParams(dimension_semantics=("arbitrary",),
                                             vmem_limit_bytes=VMEM_LIMIT_BYTES),
        name="fused_layer_safe" if safe else "fused_layer",
    )(x2d, pos3, mod3, mod3, mod3, w_in_slabs, w_out_slabs, ln_w, ln_b,
      lb_logits, hg_norm_w, dmask, zeta, xi, tribd, freq2, sign)


def kernel(x, c, positions, w_ada, b_ada, w_in, lb_logits, hg_norm_w, w_out, ln_w, ln_b):
    bsz, seq, d = x.shape
    assert w_ada.shape[0] == DEPTH and seq % RET_CHUNK == 0 and d % PIECE == 0
    mod = _ada_call(c, w_ada[0], b_ada[0])
    mod3 = mod.reshape(bsz, 1, 3 * d)
    operands = (x.reshape(bsz * seq, d), positions.reshape(bsz, seq, 1), mod3, w_in[0], w_out[0],
                ln_w[0].reshape(1, d), ln_b[0].reshape(1, d),
                lb_logits.astype(F32), hg_norm_w[0].reshape(1, HG_WIDTH))

    def run(safe, x2d, pos3, mod3_, w_in_f32, w_out_f32, *rest):
        return _fused_call(x2d, pos3, mod3_, _column_slabs(w_in_f32), _column_slabs(w_out_f32),
                           *rest, seq=seq, safe=safe)

    out = lax.cond(_needs_safe_path(lb_logits),
                   lambda ops: run(True, *ops), lambda ops: run(False, *ops), operands)
    return out.reshape(bsz, seq, d)
```

```python
import functools

import numpy as np
import jax
import jax.numpy as jnp
from jax import lax
from jax.experimental import pallas as pl
from jax.experimental.pallas import tpu as pltpu

RET_HEADS = 4
RET_DK = 128
RET_DV = 256
HG_HEADS = 8
HG_DK = 128
HG_DV = 128
RET_CHUNK = 128
HG_CHUNK = 64
ROPE_BASE = 10000.0
EPS = 1e-6
DEPTH = 1
ALPHA = (2.0 * DEPTH) ** 0.25
RET_LOG2_DECAY = tuple(5.0 + 7.0 * h / (RET_HEADS - 1) for h in range(RET_HEADS))
RET_LOG_DECAY = tuple(float(np.log1p(-np.exp2(-e))) for e in RET_LOG2_DECAY)

RET_QK = RET_HEADS * RET_DK
RET_WIDTH = RET_HEADS * RET_DV
HG_QK = HG_HEADS * HG_DK
HG_WIDTH = HG_HEADS * HG_DV
MIX_WIDTH = RET_WIDTH + HG_WIDTH
OFF_RQ = 0
OFF_RK = OFF_RQ + RET_QK
OFF_RV = OFF_RK + RET_QK
OFF_RG = OFF_RV + RET_WIDTH
OFF_HQ = OFF_RG + RET_WIDTH
OFF_HF = OFF_HQ + HG_QK
OFF_HI = OFF_HF + HG_QK
OFF_HG = OFF_HI + HG_WIDTH
IN_WIDTH = OFF_HG + HG_WIDTH
RET_GROUP = 2 * RET_DK + 2 * RET_DV
HG_GROUP = 2 * HG_DK + 2 * HG_DV

F32 = jnp.float32
BF16 = jnp.bfloat16
TOKEN_TILE = 256
PIECE = 256
V7X_VMEM_BYTES = 64 * 1024 * 1024
VMEM_LIMIT_BYTES = V7X_VMEM_BYTES - 512 * 1024
HEAD_ORDER = (("ret", 0), ("ret", 1), ("hg", 0), ("hg", 1), ("hg", 2), ("hg", 3),
              ("ret", 2), ("ret", 3), ("hg", 4), ("hg", 5), ("hg", 6), ("hg", 7))
MAX_FAST_EXPONENT = -1.0
LN_ROWS = 16
LN_BLOCKS_PER_PIECE = 8


def _slab_kernel(w_ref, o_ref):
    for j in range(o_ref.shape[0]):
        o_ref[j] = w_ref[:, pl.ds(j * PIECE, PIECE)].astype(BF16)


def _column_slabs(w):
    k, n = w.shape
    per_step = 2 if (n // PIECE) % 2 == 0 else 1
    return pl.pallas_call(
        _slab_kernel,
        grid=(n // (PIECE * per_step),),
        in_specs=[pl.BlockSpec((k, PIECE * per_step), lambda j: (0, j))],
        out_specs=pl.BlockSpec((per_step, k, PIECE), lambda j: (j, 0, 0)),
        out_shape=jax.ShapeDtypeStruct((n // PIECE, k, PIECE), BF16),
        compiler_params=pltpu.CompilerParams(dimension_semantics=("arbitrary",)),
        name="weight_slabs",
    )(w)


def _sigmoid(x):
    return 1.0 / (1.0 + jnp.exp(-x))


def _ada_kernel(c_ref, w_ref, b_ref, o_ref):
    c = c_ref[...]
    cond = c * _sigmoid(c)
    o_ref[...] = jnp.dot(cond, w_ref[...], preferred_element_type=F32) + b_ref[...]


def _ada_call(c, w, b):
    bsz, d = c.shape
    n = w.shape[1]
    tn = 512 if n % 512 == 0 else n
    return pl.pallas_call(
        _ada_kernel,
        grid=(n // tn,),
        in_specs=[pl.BlockSpec((bsz, d), lambda j: (0, 0)),
                  pl.BlockSpec((d, tn), lambda j: (0, j)),
                  pl.BlockSpec((1, tn), lambda j: (0, j))],
        out_specs=pl.BlockSpec((bsz, tn), lambda j: (0, j)),
        out_shape=jax.ShapeDtypeStruct((bsz, n), F32),
        compiler_params=pltpu.CompilerParams(dimension_semantics=("arbitrary",)),
        name="ada_mod",
    )(c, w, b.reshape(1, n))


def _dot(a, b):
    return jnp.dot(a, b, preferred_element_type=F32)


def _dot_nt(a, b):
    return lax.dot_general(a, b, (((1,), (1,)), ((), ())), preferred_element_type=F32)


def _dot_tn(a, b):
    return lax.dot_general(a, b, (((0,), (0,)), ((), ())), preferred_element_type=F32)


def _split3(x):
    hi = x.astype(BF16)
    r1 = x - hi.astype(F32)
    mid = r1.astype(BF16)
    lo = (r1 - mid.astype(F32)).astype(BF16)
    return hi, mid, lo


def _layernorm(x):
    mu = jnp.mean(x, axis=-1, keepdims=True)
    xc = x - mu
    var = jnp.mean(xc * xc, axis=-1, keepdims=True)
    return xc * lax.rsqrt(var + EPS)


def _rmsnorm(x):
    return x * lax.rsqrt(jnp.mean(x * x, axis=-1, keepdims=True) + EPS)


def _fused_kernel(xhbm_ref, pos_ref, shift_ref, scale_ref, gate_ref, win_ref, wout_ref, lnw_ref,
                  lnb_ref, lbl_ref, hgw_ref, dmask_ref, zeta_ref, xi_ref, tribd_ref, freq_ref, sign_ref,
                  o_ref, pret_ref, phg_ref, sret_ref, shg_ref, mixed_ref, xin_ref, xres_ref, h_ref,
                  hold_ref, in_sem, res_sem, *, tm, steps_per_batch, n_tiles, safe):
    s = pl.program_id(0)
    n_ret_chunks = tm // RET_CHUNK
    n_hg_chunks = tm // HG_CHUNK

    def residual_copy(step):
        tile = jnp.maximum(step - 2, 0)
        return pltpu.make_async_copy(xhbm_ref.at[pl.ds(tile * tm, tm), :], xres_ref, res_sem)

    def input_copy(step):
        tile = jnp.minimum(step, n_tiles - 1)
        return pltpu.make_async_copy(xhbm_ref.at[pl.ds(tile * tm, tm), :], xin_ref, in_sem)

    @pl.when(s == 0)
    def _():
        pret_ref[...] = jnp.zeros_like(pret_ref)
        phg_ref[...] = jnp.zeros_like(phg_ref)
        mixed_ref[...] = jnp.zeros_like(mixed_ref)
        hold_ref[...] = jnp.zeros_like(hold_ref)
        residual_copy(s).start()
        input_copy(s).start()

    @pl.when(lax.rem(jnp.maximum(s - 1, 0), steps_per_batch) == 0)
    def _():
        sret_ref[...] = jnp.zeros_like(sret_ref)
        shg_ref[...] = jnp.zeros_like(shg_ref)

    residual_copy(s).wait()
    input_copy(s).wait()

    prev_q = []
    pending = []

    def fill(n=1):
        for _ in range(n):
            while prev_q or pending:
                is_matmul, thunk = (prev_q or pending).pop(0)
                thunk()
                if is_matmul:
                    break

    def finish_previous_tile():
        while prev_q:
            fill()

    mixed_heads = set()
    deferred_pair = None if safe else (HEAD_ORDER[-1][0], HEAD_ORDER[-1][1] & ~1)

    def release(kind, h):
        mixed_heads.add((kind, h))
        if (kind, h ^ 1) in mixed_heads and (kind, h & ~1) != deferred_pair:
            pending.extend((True, p) for p in proj_pieces(kind, h & ~1, h_ref))

    def proj_pieces(kind, h0, lhs_ref):
        dst_ref = pret_ref if kind == "ret" else phg_ref

        def piece(src_col, dests):
            def run():
                r = _dot(lhs_ref[...], win_ref[src_col // PIECE])
                c = 0
                for head, col, width in dests:
                    dst_ref[head, :, pl.ds(col, width)] = r[:, c:c + width]
                    c += width
            return run

        if kind == "ret":
            pieces = [piece(off + h0 * RET_DK, [(h0, col, RET_DK), (h0 + 1, col, RET_DK)])
                      for off, col in ((OFF_RQ, 0), (OFF_RK, RET_DK))]
            pieces += [piece(off + h * RET_DV, [(h, col, RET_DV)])
                       for off, col in ((OFF_RV, 2 * RET_DK), (OFF_RG, 2 * RET_DK + RET_DV))
                       for h in (h0, h0 + 1)]
            return pieces
        return [piece(off + h0 * HG_DK, [(h0, col, HG_DK), (h0 + 1, col, HG_DK)])
                for off, col in ((OFF_HQ, 0), (OFF_HF, HG_DK), (OFF_HI, 2 * HG_DK), (OFF_HG, 2 * HG_DK + HG_DV))]

    def outproj_piece(c0):
        def run():
            o_ref[:, pl.ds(c0, PIECE)] = _dot(mixed_ref[...], wout_ref[c0 // PIECE])
        return run

    def final_layernorm():
        for r0 in range(0, tm, LN_ROWS):
            rows = pl.ds(r0, LN_ROWS)
            z = ALPHA * xres_ref[rows, :] + gate_ref[0] * o_ref[rows, :]
            o_ref[rows, :] = _layernorm(z) * lnw_ref[...] + lnb_ref[...]

        @pl.when(s + 1 < pl.num_programs(0))
        def _():
            residual_copy(s + 1).start()

    prev_q.extend((True, outproj_piece(c0)) for c0 in range(0, o_ref.shape[1], PIECE))
    prev_q.append((False, final_layernorm))
    if deferred_pair is not None:
        pending.extend((True, p) for p in proj_pieces(*deferred_pair, hold_ref))

    for i, r0 in enumerate(range(0, tm, LN_ROWS)):
        rows = pl.ds(r0, LN_ROWS)
        h_ref[rows, :] = (_layernorm(xin_ref[rows, :]) * (1.0 + scale_ref[0]) + shift_ref[0]).astype(BF16)
        if i % LN_BLOCKS_PER_PIECE == LN_BLOCKS_PER_PIECE - 1:
            fill()

    @pl.when(s + 1 < pl.num_programs(0))
    def _():
        input_copy(s + 1).start()

    ang = pos_ref[0].astype(F32) * freq_ref[...]
    cosf = jnp.cos(ang)
    sinf = jnp.sin(ang) * sign_ref[...]
    fill()

    def mix_ret(h):
        cdec = float(np.exp(RET_CHUNK * RET_LOG_DECAY[h]))
        scores, kv, qx, vs = [], [], [], []
        for c in range(n_ret_chunks):
            rows = pl.ds(c * RET_CHUNK, RET_CHUNK)
            cf = cosf[c * RET_CHUNK:(c + 1) * RET_CHUNK]
            sf = sinf[c * RET_CHUNK:(c + 1) * RET_CHUNK]
            q = pret_ref[h, rows, pl.ds(0, RET_DK)]
            k = pret_ref[h, rows, pl.ds(RET_DK, RET_DK)]
            v = pret_ref[h, rows, pl.ds(2 * RET_DK, RET_DV)].astype(BF16)
            q = (q * cf + pltpu.roll(q, RET_DK // 2, 1) * sf) * (RET_DK ** -0.5)
            k = k * cf + pltpu.roll(k, RET_DK // 2, 1) * sf
            scores.append(_dot_nt(q.astype(BF16), k.astype(BF16)))
            kv.append(_dot_tn((k * zeta_ref[h]).astype(BF16), v))
            qx.append((q * xi_ref[h]).astype(BF16))
            vs.append(v)
        fill()
        state = sret_ref[h]
        rets = []
        for c in range(n_ret_chunks):
            p = (scores[c] * dmask_ref[h]).astype(BF16)
            rets.append(_dot(p, vs[c]) + _dot(qx[c], state.astype(BF16)))
            state = cdec * state + kv[c]
        sret_ref[h] = state
        fill()
        finish_previous_tile()
        for c in range(n_ret_chunks):
            rows = pl.ds(c * RET_CHUNK, RET_CHUNK)
            g = pret_ref[h, rows, pl.ds(2 * RET_DK + RET_DV, RET_DV)]
            mixed_ref[rows, pl.ds(h * RET_DV, RET_DV)] = (_rmsnorm(rets[c]) * (g * _sigmoid(g))).astype(BF16)
        fill()
        release("ret", h)

    l0 = lbl_ref[0:1, :]
    l1 = lbl_ref[1:2, :]
    lmax = jnp.maximum(l0, l1)
    e0 = jnp.exp(l0 - lmax)
    lb_all = e0 / (e0 + jnp.exp(l1 - lmax))
    row = lax.broadcasted_iota(jnp.int32, (HG_CHUNK, HG_CHUNK), 0)
    col = lax.broadcasted_iota(jnp.int32, (HG_CHUNK, HG_CHUNK), 1)
    causal = row >= col

    def intra_chunk_scores_safe(q, k, bc):
        trow = lax.broadcasted_iota(jnp.int32, (HG_CHUNK, HG_DK), 0)
        xor = jnp.bitwise_xor(row, col)
        acc = jnp.where(row == col, jnp.sum(q * k, axis=-1, keepdims=True), 0.0)
        half = 1
        while half < HG_CHUNK:
            block = 2 * half
            upper = jnp.bitwise_and(trow, half) != 0
            if block >= 8:
                b3 = bc.reshape(HG_CHUNK // block, block, HG_DK)
                bm = jnp.broadcast_to(b3[:, half - 1:half, :], b3.shape).reshape(HG_CHUNK, HG_DK)
            elif half == 2:
                b3 = bc.reshape(HG_CHUNK // 8, 8, HG_DK)
                sub = lax.broadcasted_iota(jnp.int32, b3.shape, 1)
                bm = jnp.where(sub < 4, jnp.broadcast_to(b3[:, 1:2, :], b3.shape),
                               jnp.broadcast_to(b3[:, 5:6, :], b3.shape)).reshape(HG_CHUNK, HG_DK)
            else:
                bm = jnp.where(upper, pltpu.roll(bc, 1, 0), bc)
            u = (jnp.where(upper, q, k) * jnp.exp(-jnp.abs(bc - bm))).astype(BF16)
            level = (row > col) & (xor >= half) & (xor < block)
            acc = jnp.where(level, _dot_nt(u, u), acc)
            half = block
        return acc

    def mix_hg(h):
        lb = lb_all[:, h * HG_DK:(h + 1) * HG_DK]
        fp = phg_ref[h, :, pl.ds(HG_DK, HG_DK)]
        sg = _sigmoid(fp)
        logf = jnp.log(lb + (1.0 - lb) * sg)
        kk = (1.0 - lb) * (1.0 - sg)
        parts = _dot(tribd_ref[...], jnp.concatenate(_split3(logf), axis=1))
        b = parts[:, 0:HG_DK] + parts[:, HG_DK:2 * HG_DK] + parts[:, 2 * HG_DK:3 * HG_DK]
        fill()
        hq = phg_ref[h, :, pl.ds(0, HG_DK)]
        qd = (hq * jnp.exp(b)).astype(BF16)
        if not safe:
            kd = (kk * jnp.exp(-b)).astype(BF16)
        attn, kvt, elast, vs = [], [], [], []
        for c in range(n_hg_chunks):
            r0, r1 = c * HG_CHUNK, (c + 1) * HG_CHUNK
            b_last = b[r1 - 1:r1, :]
            ks = (kk[r0:r1] * jnp.exp(b_last - b[r0:r1])).astype(BF16)
            v = phg_ref[h, pl.ds(r0, HG_CHUNK), pl.ds(2 * HG_DK, HG_DV)].astype(BF16)
            if safe:
                attn.append(intra_chunk_scores_safe(hq[r0:r1], kk[r0:r1], b[r0:r1]))
            else:
                attn.append(jnp.where(causal, _dot_nt(qd[r0:r1], kd[r0:r1]), 0.0))
            kvt.append(_dot_tn(v, ks))
            elast.append(jnp.exp(b_last))
            vs.append(v)
        fill()
        state_t = shg_ref[h]
        outs = []
        for c in range(n_hg_chunks):
            r0, r1 = c * HG_CHUNK, (c + 1) * HG_CHUNK
            outs.append(_dot(attn[c].astype(BF16), vs[c]) + _dot_nt(qd[r0:r1], state_t.astype(BF16)))
            state_t = state_t * elast[c] + kvt[c]
        shg_ref[h] = state_t
        fill()
        finish_previous_tile()
        hgw = hgw_ref[:, pl.ds(h * HG_DV, HG_DV)]
        for c in range(n_hg_chunks):
            rows = pl.ds(c * HG_CHUNK, HG_CHUNK)
            g = phg_ref[h, rows, pl.ds(2 * HG_DK + HG_DV, HG_DV)]
            mixed_ref[rows, pl.ds(RET_WIDTH + h * HG_DV, HG_DV)] = (
                _rmsnorm(outs[c]) * hgw * (g * _sigmoid(g))).astype(BF16)
        release("hg", h)

    for kind, h in HEAD_ORDER:
        if kind == "ret":
            mix_ret(h)
        else:
            mix_hg(h)
    fill(len(pending))
    if deferred_pair is not None:
        hold_ref[...] = h_ref[...]


def _needs_safe_path(lb_logits):
    lg = lb_logits.astype(F32)
    neg_log_lb = -jax.nn.log_softmax(lg, axis=0)[0]
    return HG_CHUNK * jnp.max(neg_log_lb) > MAX_FAST_EXPONENT


def _ret_constants():
    lg = np.asarray(RET_LOG_DECAY, np.float64)
    idx = np.arange(RET_CHUNK, dtype=np.float64)
    diff = idx[:, None] - idx[None, :]
    dmask = np.where(diff >= 0, np.exp(np.where(diff >= 0, diff, 0.0)[None] * lg[:, None, None]), 0.0)
    zeta = np.exp((RET_CHUNK - 1 - idx)[None, :] * lg[:, None])
    xi = np.exp((idx + 1)[None, :] * lg[:, None])
    bc = lambda a: np.broadcast_to(a[:, :, None], (RET_HEADS, RET_CHUNK, RET_DK))
    return jnp.asarray(dmask, F32), jnp.asarray(bc(zeta), F32), jnp.asarray(bc(xi), F32)


def _chunk_tril(tm):
    i = np.arange(tm)
    m = (i[:, None] >= i[None, :]) & (i[:, None] // HG_CHUNK == i[None, :] // HG_CHUNK)
    return jnp.asarray(m, BF16)


def _fused_call(x2d, pos3, mod3, w_in_slabs, w_out_slabs, ln_w, ln_b, lb_logits, hg_norm_w, *, seq, safe):
    m, d = x2d.shape
    tm = min(TOKEN_TILE, seq)
    steps_per_batch = seq // tm
    n_tiles = m // tm
    dmask, zeta, xi = _ret_constants()
    tribd = _chunk_tril(tm)
    half = RET_DK // 2
    freqs = ROPE_BASE ** (-jnp.arange(half, dtype=F32) / half)
    freq2 = jnp.concatenate([freqs, freqs]).reshape(1, RET_DK)
    sign = jnp.concatenate([-jnp.ones((half,), F32), jnp.ones((half,), F32)]).reshape(1, RET_DK)

    last = n_tiles - 1
    cur = lambda s: jnp.minimum(s, last)
    mixd = lambda s: jnp.clip(s - 1, 0, last)
    done = lambda s: jnp.maximum(s - 2, 0)
    resident = lambda a: pl.BlockSpec(a.shape, lambda s: (0,) * a.ndim, pipeline_mode=pl.Buffered(1))
    return pl.pallas_call(
        functools.partial(_fused_kernel, tm=tm, steps_per_batch=steps_per_batch, n_tiles=n_tiles, safe=safe),
        grid=(n_tiles + 2,),
        in_specs=[pl.BlockSpec(memory_space=pl.ANY),
                  pl.BlockSpec((1, tm, 1), lambda s: (mixd(s) // steps_per_batch, mixd(s) % steps_per_batch, 0)),
                  pl.BlockSpec((1, 1, d), lambda s: (cur(s) // steps_per_batch, 0, 0)),
                  pl.BlockSpec((1, 1, d), lambda s: (cur(s) // steps_per_batch, 0, 1)),
                  pl.BlockSpec((1, 1, d), lambda s: (done(s) // steps_per_batch, 0, 2)),
                  resident(w_in_slabs), resident(w_out_slabs), resident(ln_w), resident(ln_b),
                  resident(lb_logits), resident(hg_norm_w), resident(dmask), resident(zeta),
                  resident(xi), resident(tribd), resident(freq2), resident(sign)],
        out_specs=pl.BlockSpec((tm, d), lambda s: (done(s), 0), pipeline_mode=pl.Buffered(1)),
        out_shape=jax.ShapeDtypeStruct((m, d), F32),
        scratch_shapes=[pltpu.VMEM((RET_HEADS, tm, RET_GROUP), F32),
                        pltpu.VMEM((HG_HEADS, tm, HG_GROUP), F32),
                        pltpu.VMEM((RET_HEADS, RET_DK, RET_DV), F32),
                        pltpu.VMEM((HG_HEADS, HG_DV, HG_DK), F32),
                        pltpu.VMEM((tm, MIX_WIDTH), BF16),
                        pltpu.VMEM((tm, d), F32),
                        pltpu.VMEM((tm, d), F32),
                        pltpu.VMEM((tm, d), BF16),
                        pltpu.VMEM((16, 128) if safe else (tm, d), BF16),
                        pltpu.SemaphoreType.DMA(()),
                        pltpu.SemaphoreType.DMA(())],
        compiler_params=pltpu.CompilerParams(dimension_semantics=("arbitrary",),
                                             vmem_limit_bytes=VMEM_LIMIT_BYTES),
        name="fused_layer_safe" if safe else "fused_layer",
    )(x2d, pos3, mod3, mod3, mod3, w_in_slabs, w_out_slabs, ln_w, ln_b,
      lb_logits, hg_norm_w, dmask, zeta, xi, tribd, freq2, sign)


def kernel(x, c, positions, w_ada, b_ada, w_in, lb_logits, hg_norm_w, w_out, ln_w, ln_b):
    bsz, seq, d = x.shape
    assert w_ada.shape[0] == DEPTH and seq % RET_CHUNK == 0 and d % PIECE == 0
    mod = _ada_call(c, w_ada[0], b_ada[0])
    mod3 = mod.reshape(bsz, 1, 3 * d)
    operands = (x.reshape(bsz * seq, d), positions.reshape(bsz, seq, 1), mod3, w_in[0], w_out[0],
                ln_w[0].reshape(1, d), ln_b[0].reshape(1, d),
                lb_logits.astype(F32), hg_norm_w[0].reshape(1, HG_WIDTH))

    def run(safe, x2d, pos3, mod3_, w_in_f32, w_out_f32, *rest):
        return _fused_call(x2d, pos3, mod3_, _column_slabs(w_in_f32), _column_slabs(w_out_f32),
                           *rest, seq=seq, safe=safe)

    out = lax.cond(_needs_safe_path(lb_logits),
                   lambda ops: run(True, *ops), lambda ops: run(False, *ops), operands)
    return out.reshape(bsz, seq, d)
```

```python
import functools

import numpy as np
import jax
import jax.numpy as jnp
from jax import lax
from jax.experimental import pallas as pl
from jax.experimental.pallas import tpu as pltpu

RET_HEADS = 4
RET_DK = 128
RET_DV = 256
HG_HEADS = 8
HG_DK = 128
HG_DV = 128
RET_CHUNK = 128
HG_CHUNK = 64
ROPE_BASE = 10000.0
EPS = 1e-6
DEPTH = 1
ALPHA = (2.0 * DEPTH) ** 0.25
RET_LOG2_DECAY = tuple(5.0 + 7.0 * h / (RET_HEADS - 1) for h in range(RET_HEADS))
RET_LOG_DECAY = tuple(float(np.log1p(-np.exp2(-e))) for e in RET_LOG2_DECAY)

RET_QK = RET_HEADS * RET_DK
RET_WIDTH = RET_HEADS * RET_DV
HG_QK = HG_HEADS * HG_DK
HG_WIDTH = HG_HEADS * HG_DV
MIX_WIDTH = RET_WIDTH + HG_WIDTH
OFF_RQ = 0
OFF_RK = OFF_RQ + RET_QK
OFF_RV = OFF_RK + RET_QK
OFF_RG = OFF_RV + RET_WIDTH
OFF_HQ = OFF_RG + RET_WIDTH
OFF_HF = OFF_HQ + HG_QK
OFF_HI = OFF_HF + HG_QK
OFF_HG = OFF_HI + HG_WIDTH
IN_WIDTH = OFF_HG + HG_WIDTH
RET_GROUP = 2 * RET_DK + 2 * RET_DV
HG_GROUP = 2 * HG_DK + 2 * HG_DV

F32 = jnp.float32
BF16 = jnp.bfloat16
TOKEN_TILE = 256
SAFE_TOKEN_TILE = 128
PIECE = 256
SLABS_PER_PIECE = 1
V7X_VMEM_BYTES = 64 * 1024 * 1024
VMEM_LIMIT_BYTES = V7X_VMEM_BYTES - 512 * 1024
HEAD_ORDER = (("ret", 0), ("ret", 1), ("hg", 0), ("hg", 1), ("hg", 2), ("hg", 3),
              ("ret", 2), ("ret", 3), ("hg", 4), ("hg", 5), ("hg", 6), ("hg", 7))
MAX_FAST_EXPONENT = 80.0
LN_ROWS = 16
LN_BLOCKS_PER_PIECE = 8


def _slab_kernel(w_ref, o_ref):
    for j in range(o_ref.shape[0]):
        o_ref[j] = w_ref[:, pl.ds(j * PIECE, PIECE)].astype(BF16)


def _column_slabs(w):
    k, n = w.shape
    per_step = 2 if (n // PIECE) % 2 == 0 else 1
    return pl.pallas_call(
        _slab_kernel,
        grid=(n // (PIECE * per_step),),
        in_specs=[pl.BlockSpec((k, PIECE * per_step), lambda j: (0, j))],
        out_specs=pl.BlockSpec((per_step, k, PIECE), lambda j: (j, 0, 0)),
        out_shape=jax.ShapeDtypeStruct((n // PIECE, k, PIECE), BF16),
        compiler_params=pltpu.CompilerParams(dimension_semantics=("arbitrary",)),
        name="weight_slabs",
    )(w)


def _sigmoid(x):
    return 1.0 / (1.0 + jnp.exp(-x))


def _ada_kernel(c_ref, w_ref, b_ref, o_ref):
    c = c_ref[...]
    cond = c * _sigmoid(c)
    o_ref[...] = jnp.dot(cond, w_ref[...], preferred_element_type=F32) + b_ref[...]


def _ada_call(c, w, b):
    bsz, d = c.shape
    n = w.shape[1]
    tn = 512 if n % 512 == 0 else n
    return pl.pallas_call(
        _ada_kernel,
        grid=(n // tn,),
        in_specs=[pl.BlockSpec((bsz, d), lambda j: (0, 0)),
                  pl.BlockSpec((d, tn), lambda j: (0, j)),
                  pl.BlockSpec((1, tn), lambda j: (0, j))],
        out_specs=pl.BlockSpec((bsz, tn), lambda j: (0, j)),
        out_shape=jax.ShapeDtypeStruct((bsz, n), F32),
        compiler_params=pltpu.CompilerParams(dimension_semantics=("arbitrary",)),
        name="ada_mod",
    )(c, w, b.reshape(1, n))


def _dot(a, b):
    return jnp.dot(a, b, preferred_element_type=F32)


def _dot_nt(a, b):
    return lax.dot_general(a, b, (((1,), (1,)), ((), ())), preferred_element_type=F32)


def _dot_tn(a, b):
    return lax.dot_general(a, b, (((0,), (0,)), ((), ())), preferred_element_type=F32)


def _layernorm(x):
    mu = jnp.mean(x, axis=-1, keepdims=True)
    xc = x - mu
    var = jnp.mean(xc * xc, axis=-1, keepdims=True)
    return xc * lax.rsqrt(var + EPS)


def _rmsnorm(x):
    return x * lax.rsqrt(jnp.mean(x * x, axis=-1, keepdims=True) + EPS)


def _fused_kernel(xhbm_ref, pos_ref, shift_ref, scale_ref, gate_ref, win_ref, wout_ref, lnw_ref,
                  lnb_ref, lbl_ref, hgw_ref, dmask_ref, zeta_ref, xi_ref, freq_ref, sign_ref,
                  o_ref, pret_ref, phg_ref, sret_ref, shg_ref, mixed_ref, xin_ref, xres_ref, h_ref,
                  hold_ref, in_sem, res_sem, *, tm, steps_per_batch, n_tiles, safe):
    s = pl.program_id(0)
    n_ret_chunks = tm // RET_CHUNK
    n_hg_chunks = tm // HG_CHUNK

    def residual_copy(step):
        tile = jnp.maximum(step - 2, 0)
        return pltpu.make_async_copy(xhbm_ref.at[pl.ds(tile * tm, tm), :], xres_ref, res_sem)

    def input_copy(step):
        tile = jnp.minimum(step, n_tiles - 1)
        return pltpu.make_async_copy(xhbm_ref.at[pl.ds(tile * tm, tm), :], xin_ref, in_sem)

    @pl.when(s == 0)
    def _():
        pret_ref[...] = jnp.zeros_like(pret_ref)
        phg_ref[...] = jnp.zeros_like(phg_ref)
        mixed_ref[...] = jnp.zeros_like(mixed_ref)
        hold_ref[...] = jnp.zeros_like(hold_ref)
        residual_copy(s).start()
        input_copy(s).start()

    @pl.when(lax.rem(jnp.maximum(s - 1, 0), steps_per_batch) == 0)
    def _():
        sret_ref[...] = jnp.zeros_like(sret_ref)
        shg_ref[...] = jnp.zeros_like(shg_ref)

    residual_copy(s).wait()
    input_copy(s).wait()

    prev_q = []
    pending = []

    def fill(n=1):
        for _ in range(n):
            while prev_q or pending:
                is_matmul, thunk = (prev_q or pending).pop(0)
                thunk()
                if is_matmul:
                    break

    def finish_previous_tile():
        while prev_q:
            fill()

    mixed_heads = set()
    deferred_pair = None if safe else (HEAD_ORDER[-1][0], HEAD_ORDER[-1][1] & ~1)

    def release(kind, h):
        mixed_heads.add((kind, h))
        if (kind, h ^ 1) in mixed_heads and (kind, h & ~1) != deferred_pair:
            pending.extend((True, p) for p in proj_pieces(kind, h & ~1, h_ref))

    def proj_pieces(kind, h0, lhs_ref):
        dst_ref = pret_ref if kind == "ret" else phg_ref

        def piece(specs):
            def run():
                w = [win_ref[src_col // PIECE] for src_col, _ in specs]
                r = _dot(lhs_ref[...], w[0] if len(w) == 1 else jnp.concatenate(w, axis=1))
                c = 0
                for _, dests in specs:
                    for head, col, width in dests:
                        dst_ref[head, :, pl.ds(col, width)] = r[:, c:c + width]
                        c += width
            return run

        if kind == "ret":
            specs = [(off + h0 * RET_DK, [(h0, col, RET_DK), (h0 + 1, col, RET_DK)])
                     for off, col in ((OFF_RQ, 0), (OFF_RK, RET_DK))]
            specs += [(off + h * RET_DV, [(h, col, RET_DV)])
                      for off, col in ((OFF_RV, 2 * RET_DK), (OFF_RG, 2 * RET_DK + RET_DV))
                      for h in (h0, h0 + 1)]
        else:
            specs = [(off + h0 * HG_DK, [(h0, col, HG_DK), (h0 + 1, col, HG_DK)])
                     for off, col in ((OFF_HQ, 0), (OFF_HF, HG_DK), (OFF_HI, 2 * HG_DK), (OFF_HG, 2 * HG_DK + HG_DV))]
        return [piece(specs[i:i + SLABS_PER_PIECE]) for i in range(0, len(specs), SLABS_PER_PIECE)]

    def outproj_piece(c0):
        def run():
            w = [wout_ref[c0 // PIECE + j] for j in range(SLABS_PER_PIECE)]
            o_ref[:, pl.ds(c0, PIECE * SLABS_PER_PIECE)] = _dot(
                mixed_ref[...], w[0] if len(w) == 1 else jnp.concatenate(w, axis=1))
        return run

    def final_layernorm():
        for r0 in range(0, tm, LN_ROWS):
            rows = pl.ds(r0, LN_ROWS)
            z = ALPHA * xres_ref[rows, :] + gate_ref[0] * o_ref[rows, :]
            o_ref[rows, :] = _layernorm(z) * lnw_ref[...] + lnb_ref[...]

        @pl.when(s + 1 < pl.num_programs(0))
        def _():
            residual_copy(s + 1).start()

    prev_q.extend((True, outproj_piece(c0)) for c0 in range(0, o_ref.shape[1], PIECE * SLABS_PER_PIECE))
    prev_q.append((False, final_layernorm))
    if deferred_pair is not None:
        pending.extend((True, p) for p in proj_pieces(*deferred_pair, hold_ref))

    for i, r0 in enumerate(range(0, tm, LN_ROWS)):
        rows = pl.ds(r0, LN_ROWS)
        h_ref[rows, :] = (_layernorm(xin_ref[rows, :]) * (1.0 + scale_ref[0]) + shift_ref[0]).astype(BF16)
        if i % LN_BLOCKS_PER_PIECE == LN_BLOCKS_PER_PIECE - 1:
            fill()

    @pl.when(s + 1 < pl.num_programs(0))
    def _():
        input_copy(s + 1).start()

    ang = pos_ref[0].astype(F32) * freq_ref[...]
    cosf = jnp.cos(ang)
    sinf = jnp.sin(ang) * sign_ref[...]
    fill()

    def mix_ret(h):
        cdec = float(np.exp(RET_CHUNK * RET_LOG_DECAY[h]))
        scores, kv, qx, vs = [], [], [], []
        for c in range(n_ret_chunks):
            rows = pl.ds(c * RET_CHUNK, RET_CHUNK)
            cf = cosf[c * RET_CHUNK:(c + 1) * RET_CHUNK]
            sf = sinf[c * RET_CHUNK:(c + 1) * RET_CHUNK]
            q = pret_ref[h, rows, pl.ds(0, RET_DK)]
            k = pret_ref[h, rows, pl.ds(RET_DK, RET_DK)]
            v = pret_ref[h, rows, pl.ds(2 * RET_DK, RET_DV)].astype(BF16)
            q = (q * cf + pltpu.roll(q, RET_DK // 2, 1) * sf) * (RET_DK ** -0.5)
            k = k * cf + pltpu.roll(k, RET_DK // 2, 1) * sf
            scores.append(_dot_nt(q.astype(BF16), k.astype(BF16)))
            kv.append(_dot_tn((k * zeta_ref[h]).astype(BF16), v))
            qx.append((q * xi_ref[h]).astype(BF16))
            vs.append(v)
        fill()
        state = sret_ref[h]
        rets = []
        for c in range(n_ret_chunks):
            p = (scores[c] * dmask_ref[h]).astype(BF16)
            rets.append(_dot(p, vs[c]) + _dot(qx[c], state.astype(BF16)))
            state = cdec * state + kv[c]
        sret_ref[h] = state
        fill()
        finish_previous_tile()
        for c in range(n_ret_chunks):
            rows = pl.ds(c * RET_CHUNK, RET_CHUNK)
            g = pret_ref[h, rows, pl.ds(2 * RET_DK + RET_DV, RET_DV)]
            mixed_ref[rows, pl.ds(h * RET_DV, RET_DV)] = (_rmsnorm(rets[c]) * (g * _sigmoid(g))).astype(BF16)
        fill()
        release("ret", h)

    l0 = lbl_ref[0:1, :]
    l1 = lbl_ref[1:2, :]
    lmax = jnp.maximum(l0, l1)
    e0 = jnp.exp(l0 - lmax)
    lb_all = e0 / (e0 + jnp.exp(l1 - lmax))
    row = lax.broadcasted_iota(jnp.int32, (HG_CHUNK, HG_CHUNK), 0)
    col = lax.broadcasted_iota(jnp.int32, (HG_CHUNK, HG_CHUNK), 1)
    causal = row >= col

    def intra_chunk_scores_safe(q, k, bc):
        trow = lax.broadcasted_iota(jnp.int32, (HG_CHUNK, HG_DK), 0)
        xor = jnp.bitwise_xor(row, col)
        acc = jnp.where(row == col, jnp.sum(q * k, axis=-1, keepdims=True), 0.0)
        half = 1
        while half < HG_CHUNK:
            block = 2 * half
            upper = jnp.bitwise_and(trow, half) != 0
            if block >= 8:
                b3 = bc.reshape(HG_CHUNK // block, block, HG_DK)
                bm = jnp.broadcast_to(b3[:, half - 1:half, :], b3.shape).reshape(HG_CHUNK, HG_DK)
            elif half == 2:
                b3 = bc.reshape(HG_CHUNK // 8, 8, HG_DK)
                sub = lax.broadcasted_iota(jnp.int32, b3.shape, 1)
                bm = jnp.where(sub < 4, jnp.broadcast_to(b3[:, 1:2, :], b3.shape),
                               jnp.broadcast_to(b3[:, 5:6, :], b3.shape)).reshape(HG_CHUNK, HG_DK)
            else:
                bm = jnp.where(upper, pltpu.roll(bc, 1, 0), bc)
            u = (jnp.where(upper, q, k) * jnp.exp(-jnp.abs(bc - bm))).astype(BF16)
            level = (row > col) & (xor >= half) & (xor < block)
            acc = jnp.where(level, _dot_nt(u, u), acc)
            half = block
        return acc

    sublane = jnp.bitwise_and(lax.broadcasted_iota(jnp.int32, (tm, HG_DK), 0), 7)

    def chunk_cumsum(x):
        s8 = x
        for k in (1, 2, 4):
            s8 = s8 + jnp.where(sublane >= k, pltpu.roll(s8, k, 0), 0.0)
        blocks = []
        for c in range(n_hg_chunks):
            carry = None
            for j in range(HG_CHUNK // 8):
                r0 = c * HG_CHUNK + 8 * j
                blk = s8[r0:r0 + 8] if carry is None else s8[r0:r0 + 8] + carry
                blocks.append(blk)
                carry = jnp.broadcast_to(blk[7:8], blk.shape)
        return jnp.concatenate(blocks, axis=0)

    def mix_hg(h):
        lb = lb_all[:, h * HG_DK:(h + 1) * HG_DK]
        fp = phg_ref[h, :, pl.ds(HG_DK, HG_DK)]
        sg = _sigmoid(fp)
        logf = jnp.log(lb + (1.0 - lb) * sg)
        kk = (1.0 - lb) * (1.0 - sg)
        b = chunk_cumsum(logf)
        fill()
        hq = phg_ref[h, :, pl.ds(0, HG_DK)]
        qd = (hq * jnp.exp(b)).astype(BF16)
        if not safe:
            kd = (kk * jnp.exp(-b)).astype(BF16)
        attn, kvt, elast, vs = [], [], [], []
        for c in range(n_hg_chunks):
            r0, r1 = c * HG_CHUNK, (c + 1) * HG_CHUNK
            b_last = b[r1 - 1:r1, :]
            ks = (kk[r0:r1] * jnp.exp(b_last - b[r0:r1])).astype(BF16)
            v = phg_ref[h, pl.ds(r0, HG_CHUNK), pl.ds(2 * HG_DK, HG_DV)].astype(BF16)
            if safe:
                attn.append(intra_chunk_scores_safe(hq[r0:r1], kk[r0:r1], b[r0:r1]))
            else:
                attn.append(jnp.where(causal, _dot_nt(qd[r0:r1], kd[r0:r1]), 0.0))
            kvt.append(_dot_tn(v, ks))
            elast.append(jnp.exp(b_last))
            vs.append(v)
        fill()
        state_t = shg_ref[h]
        outs = []
        for c in range(n_hg_chunks):
            r0, r1 = c * HG_CHUNK, (c + 1) * HG_CHUNK
            outs.append(_dot(attn[c].astype(BF16), vs[c]) + _dot_nt(qd[r0:r1], state_t.astype(BF16)))
            state_t = state_t * elast[c] + kvt[c]
        shg_ref[h] = state_t
        fill()
        finish_previous_tile()
        hgw = hgw_ref[:, pl.ds(h * HG_DV, HG_DV)]
        for c in range(n_hg_chunks):
            rows = pl.ds(c * HG_CHUNK, HG_CHUNK)
            g = phg_ref[h, rows, pl.ds(2 * HG_DK + HG_DV, HG_DV)]
            mixed_ref[rows, pl.ds(RET_WIDTH + h * HG_DV, HG_DV)] = (
                _rmsnorm(outs[c]) * hgw * (g * _sigmoid(g))).astype(BF16)
        release("hg", h)

    for kind, h in HEAD_ORDER:
        if kind == "ret":
            mix_ret(h)
        else:
            mix_hg(h)
    fill(len(pending))
    if deferred_pair is not None:
        hold_ref[...] = h_ref[...]


def _needs_safe_path(lb_logits):
    lg = lb_logits.astype(F32)
    neg_log_lb = -jax.nn.log_softmax(lg, axis=0)[0]
    return HG_CHUNK * jnp.max(neg_log_lb) > MAX_FAST_EXPONENT


def _ret_constants():
    lg = np.asarray(RET_LOG_DECAY, np.float64)
    idx = np.arange(RET_CHUNK, dtype=np.float64)
    diff = idx[:, None] - idx[None, :]
    dmask = np.where(diff >= 0, np.exp(np.where(diff >= 0, diff, 0.0)[None] * lg[:, None, None]), 0.0)
    zeta = np.exp((RET_CHUNK - 1 - idx)[None, :] * lg[:, None])
    xi = np.exp((idx + 1)[None, :] * lg[:, None])
    bc = lambda a: np.broadcast_to(a[:, :, None], (RET_HEADS, RET_CHUNK, RET_DK))
    return jnp.asarray(dmask, F32), jnp.asarray(bc(zeta), F32), jnp.asarray(bc(xi), F32)


def _fused_call(x2d, pos3, mod3, w_in_slabs, w_out_slabs, ln_w, ln_b, lb_logits, hg_norm_w, *, seq, safe):
    m, d = x2d.shape
    tm = min(SAFE_TOKEN_TILE if safe else TOKEN_TILE, seq)
    steps_per_batch = seq // tm
    n_tiles = m // tm
    dmask, zeta, xi = _ret_constants()
    half = RET_DK // 2
    freqs = ROPE_BASE ** (-jnp.arange(half, dtype=F32) / half)
    freq2 = jnp.concatenate([freqs, freqs]).reshape(1, RET_DK)
    sign = jnp.concatenate([-jnp.ones((half,), F32), jnp.ones((half,), F32)]).reshape(1, RET_DK)

    last = n_tiles - 1
    cur = lambda s: jnp.minimum(s, last)
    mixd = lambda s: jnp.clip(s - 1, 0, last)
    done = lambda s: jnp.maximum(s - 2, 0)
    resident = lambda a: pl.BlockSpec(a.shape, lambda s: (0,) * a.ndim, pipeline_mode=pl.Buffered(1))
    return pl.pallas_call(
        functools.partial(_fused_kernel, tm=tm, steps_per_batch=steps_per_batch, n_tiles=n_tiles, safe=safe),
        grid=(n_tiles + 2,),
        in_specs=[pl.BlockSpec(memory_space=pl.ANY),
                  pl.BlockSpec((1, tm, 1), lambda s: (mixd(s) // steps_per_batch, mixd(s) % steps_per_batch, 0)),
                  pl.BlockSpec((1, 1, d), lambda s: (cur(s) // steps_per_batch, 0, 0)),
                  pl.BlockSpec((1, 1, d), lambda s: (cur(s) // steps_per_batch, 0, 1)),
                  pl.BlockSpec((1, 1, d), lambda s: (done(s) // steps_per_batch, 0, 2)),
                  resident(w_in_slabs), resident(w_out_slabs), resident(ln_w), resident(ln_b),
                  resident(lb_logits), resident(hg_norm_w), resident(dmask), resident(zeta),
                  resident(xi), resident(freq2), resident(sign)],
        out_specs=pl.BlockSpec((tm, d), lambda s: (done(s), 0), pipeline_mode=pl.Buffered(1)),
        out_shape=jax.ShapeDtypeStruct((m, d), F32),
        scratch_shapes=[pltpu.VMEM((RET_HEADS, tm, RET_GROUP), F32),
                        pltpu.VMEM((HG_HEADS, tm, HG_GROUP), F32),
                        pltpu.VMEM((RET_HEADS, RET_DK, RET_DV), F32),
                        pltpu.VMEM((HG_HEADS, HG_DV, HG_DK), F32),
                        pltpu.VMEM((tm, MIX_WIDTH), BF16),
                        pltpu.VMEM((tm, d), F32),
                        pltpu.VMEM((tm, d), F32),
                        pltpu.VMEM((tm, d), BF16),
                        pltpu.VMEM((16, 128) if safe else (tm, d), BF16),
                        pltpu.SemaphoreType.DMA(()),
                        pltpu.SemaphoreType.DMA(())],
        compiler_params=pltpu.CompilerParams(dimension_semantics=("arbitrary",),
                                             vmem_limit_bytes=VMEM_LIMIT_BYTES),
        name="fused_layer_safe" if safe else "fused_layer",
    )(x2d, pos3, mod3, mod3, mod3, w_in_slabs, w_out_slabs, ln_w, ln_b,
      lb_logits, hg_norm_w, dmask, zeta, xi, freq2, sign)


def kernel(x, c, positions, w_ada, b_ada, w_in, lb_logits, hg_norm_w, w_out, ln_w, ln_b):
    bsz, seq, d = x.shape
    assert w_ada.shape[0] == DEPTH and seq % RET_CHUNK == 0 and d % PIECE == 0
    mod = _ada_call(c, w_ada[0], b_ada[0])
    mod3 = mod.reshape(bsz, 1, 3 * d)
    operands = (x.reshape(bsz * seq, d), positions.reshape(bsz, seq, 1), mod3, w_in[0], w_out[0],
                ln_w[0].reshape(1, d), ln_b[0].reshape(1, d),
                lb_logits.astype(F32), hg_norm_w[0].reshape(1, HG_WIDTH))

    def run(safe, x2d, pos3, mod3_, w_in_f32, w_out_f32, *rest):
        return _fused_call(x2d, pos3, mod3_, _column_slabs(w_in_f32), _column_slabs(w_out_f32),
                           *rest, seq=seq, safe=safe)

    out = lax.cond(_needs_safe_path(lb_logits),
                   lambda ops: run(True, *ops), lambda ops: run(False, *ops), operands)
    return out.reshape(bsz, seq, d)
```

```python
import functools

import numpy as np
import jax
import jax.numpy as jnp
from jax import lax
from jax.experimental import pallas as pl
from jax.experimental.pallas import tpu as pltpu

RET_HEADS = 4
RET_DK = 128
RET_DV = 256
HG_HEADS = 8
HG_DK = 128
HG_DV = 128
RET_CHUNK = 128
HG_CHUNK = 64
ROPE_BASE = 10000.0
EPS = 1e-6
DEPTH = 1
ALPHA = (2.0 * DEPTH) ** 0.25
RET_LOG2_DECAY = tuple(5.0 + 7.0 * h / (RET_HEADS - 1) for h in range(RET_HEADS))
RET_LOG_DECAY = tuple(float(np.log1p(-np.exp2(-e))) for e in RET_LOG2_DECAY)

RET_QK = RET_HEADS * RET_DK
RET_WIDTH = RET_HEADS * RET_DV
HG_QK = HG_HEADS * HG_DK
HG_WIDTH = HG_HEADS * HG_DV
MIX_WIDTH = RET_WIDTH + HG_WIDTH
OFF_RQ = 0
OFF_RK = OFF_RQ + RET_QK
OFF_RV = OFF_RK + RET_QK
OFF_RG = OFF_RV + RET_WIDTH
OFF_HQ = OFF_RG + RET_WIDTH
OFF_HF = OFF_HQ + HG_QK
OFF_HI = OFF_HF + HG_QK
OFF_HG = OFF_HI + HG_WIDTH
IN_WIDTH = OFF_HG + HG_WIDTH
RET_GROUP = 2 * RET_DK + 2 * RET_DV
HG_GROUP = 2 * HG_DK + 2 * HG_DV

F32 = jnp.float32
BF16 = jnp.bfloat16
TOKEN_TILE = 256
SAFE_TOKEN_TILE = 128
PIECE = 256
V7X_VMEM_BYTES = 64 * 1024 * 1024
VMEM_LIMIT_BYTES = V7X_VMEM_BYTES - 512 * 1024
HEAD_ORDER = (("ret", 0), ("ret", 1), ("hg", 0), ("hg", 1), ("hg", 2), ("hg", 3),
              ("ret", 2), ("ret", 3), ("hg", 4), ("hg", 5), ("hg", 6), ("hg", 7))
MAX_FAST_EXPONENT = -1.0
LN_ROWS = 16
LN_BLOCKS_PER_PIECE = 8


def _slab_kernel(w_ref, o_ref):
    for j in range(o_ref.shape[0]):
        o_ref[j] = w_ref[:, pl.ds(j * PIECE, PIECE)].astype(BF16)


def _column_slabs(w):
    k, n = w.shape
    per_step = 2 if (n // PIECE) % 2 == 0 else 1
    return pl.pallas_call(
        _slab_kernel,
        grid=(n // (PIECE * per_step),),
        in_specs=[pl.BlockSpec((k, PIECE * per_step), lambda j: (0, j))],
        out_specs=pl.BlockSpec((per_step, k, PIECE), lambda j: (j, 0, 0)),
        out_shape=jax.ShapeDtypeStruct((n // PIECE, k, PIECE), BF16),
        compiler_params=pltpu.CompilerParams(dimension_semantics=("arbitrary",)),
        name="weight_slabs",
    )(w)


def _sigmoid(x):
    return 1.0 / (1.0 + jnp.exp(-x))


def _ada_kernel(c_ref, w_ref, b_ref, o_ref):
    c = c_ref[...]
    cond = c * _sigmoid(c)
    o_ref[...] = jnp.dot(cond, w_ref[...], preferred_element_type=F32) + b_ref[...]


def _ada_call(c, w, b):
    bsz, d = c.shape
    n = w.shape[1]
    tn = 1024 if n % 1024 == 0 else n
    return pl.pallas_call(
        _ada_kernel,
        grid=(n // tn,),
        in_specs=[pl.BlockSpec((bsz, d), lambda j: (0, 0)),
                  pl.BlockSpec((d, tn), lambda j: (0, j)),
                  pl.BlockSpec((1, tn), lambda j: (0, j))],
        out_specs=pl.BlockSpec((bsz, tn), lambda j: (0, j)),
        out_shape=jax.ShapeDtypeStruct((bsz, n), F32),
        compiler_params=pltpu.CompilerParams(dimension_semantics=("arbitrary",)),
        name="ada_mod",
    )(c, w, b.reshape(1, n))


def _dot(a, b):
    return jnp.dot(a, b, preferred_element_type=F32)


def _dot_nt(a, b):
    return lax.dot_general(a, b, (((1,), (1,)), ((), ())), preferred_element_type=F32)


def _dot_tn(a, b):
    return lax.dot_general(a, b, (((0,), (0,)), ((), ())), preferred_element_type=F32)


def _layernorm(x):
    mu = jnp.mean(x, axis=-1, keepdims=True)
    xc = x - mu
    var = jnp.mean(xc * xc, axis=-1, keepdims=True)
    return xc * lax.rsqrt(var + EPS)


def _rmsnorm(x):
    return x * lax.rsqrt(jnp.mean(x * x, axis=-1, keepdims=True) + EPS)


def _fused_kernel(xhbm_ref, pos_ref, shift_ref, scale_ref, gate_ref, win_ref, wout_ref, lnw_ref,
                  lnb_ref, lbl_ref, hgw_ref, dmask_ref, zeta_ref, xi_ref, freq_ref, sign_ref,
                  o_ref, pret_ref, phg_ref, sret_ref, shg_ref, mixed_ref, xin_ref, xres_ref, h_ref,
                  hold_ref, in_sem, res_sem, *, tm, steps_per_batch, n_tiles, safe):
    s = pl.program_id(0)
    n_ret_chunks = tm // RET_CHUNK
    n_hg_chunks = tm // HG_CHUNK

    def residual_copy(step):
        tile = jnp.maximum(step - 2, 0)
        return pltpu.make_async_copy(xhbm_ref.at[pl.ds(tile * tm, tm), :], xres_ref, res_sem)

    def input_copy(step):
        tile = jnp.minimum(step, n_tiles - 1)
        return pltpu.make_async_copy(xhbm_ref.at[pl.ds(tile * tm, tm), :], xin_ref, in_sem)

    @pl.when(s == 0)
    def _():
        pret_ref[...] = jnp.zeros_like(pret_ref)
        phg_ref[...] = jnp.zeros_like(phg_ref)
        mixed_ref[...] = jnp.zeros_like(mixed_ref)
        hold_ref[...] = jnp.zeros_like(hold_ref)
        residual_copy(s).start()
        input_copy(s).start()

    @pl.when(lax.rem(jnp.maximum(s - 1, 0), steps_per_batch) == 0)
    def _():
        sret_ref[...] = jnp.zeros_like(sret_ref)
        shg_ref[...] = jnp.zeros_like(shg_ref)

    residual_copy(s).wait()
    input_copy(s).wait()

    prev_q = []
    pending = []

    def fill(n=1):
        for _ in range(n):
            while prev_q or pending:
                is_matmul, thunk = (prev_q or pending).pop(0)
                thunk()
                if is_matmul:
                    break

    def finish_previous_tile():
        while prev_q:
            fill()

    mixed_heads = set()
    deferred_pair = None if safe else (HEAD_ORDER[-1][0], HEAD_ORDER[-1][1] & ~1)

    def release(kind, h):
        mixed_heads.add((kind, h))
        if (kind, h ^ 1) in mixed_heads and (kind, h & ~1) != deferred_pair:
            pending.extend((True, p) for p in proj_pieces(kind, h & ~1, h_ref))

    def proj_pieces(kind, h0, lhs_ref):
        dst_ref = pret_ref if kind == "ret" else phg_ref

        def piece(src_col, dests):
            def run():
                r = _dot(lhs_ref[...], win_ref[src_col // PIECE])
                c = 0
                for head, col, width in dests:
                    dst_ref[head, :, pl.ds(col, width)] = r[:, c:c + width]
                    c += width
            return run

        if kind == "ret":
            pieces = [piece(off + h0 * RET_DK, [(h0, col, RET_DK), (h0 + 1, col, RET_DK)])
                      for off, col in ((OFF_RQ, 0), (OFF_RK, RET_DK))]
            pieces += [piece(off + h * RET_DV, [(h, col, RET_DV)])
                       for off, col in ((OFF_RV, 2 * RET_DK), (OFF_RG, 2 * RET_DK + RET_DV))
                       for h in (h0, h0 + 1)]
            return pieces
        return [piece(off + h0 * HG_DK, [(h0, col, HG_DK), (h0 + 1, col, HG_DK)])
                for off, col in ((OFF_HQ, 0), (OFF_HF, HG_DK), (OFF_HI, 2 * HG_DK), (OFF_HG, 2 * HG_DK + HG_DV))]

    def outproj_piece(c0):
        def run():
            o_ref[:, pl.ds(c0, PIECE)] = _dot(mixed_ref[...], wout_ref[c0 // PIECE])
        return run

    def final_layernorm():
        for r0 in range(0, tm, LN_ROWS):
            rows = pl.ds(r0, LN_ROWS)
            z = ALPHA * xres_ref[rows, :] + gate_ref[0] * o_ref[rows, :]
            o_ref[rows, :] = _layernorm(z) * lnw_ref[...] + lnb_ref[...]

        @pl.when(s + 1 < pl.num_programs(0))
        def _():
            residual_copy(s + 1).start()

    prev_q.extend((True, outproj_piece(c0)) for c0 in range(0, o_ref.shape[1], PIECE))
    prev_q.append((False, final_layernorm))
    if deferred_pair is not None:
        pending.extend((True, p) for p in proj_pieces(*deferred_pair, hold_ref))

    for i, r0 in enumerate(range(0, tm, LN_ROWS)):
        rows = pl.ds(r0, LN_ROWS)
        h_ref[rows, :] = (_layernorm(xin_ref[rows, :]) * (1.0 + scale_ref[0]) + shift_ref[0]).astype(BF16)
        if i % LN_BLOCKS_PER_PIECE == LN_BLOCKS_PER_PIECE - 1:
            fill()

    @pl.when(s + 1 < pl.num_programs(0))
    def _():
        input_copy(s + 1).start()

    ang = pos_ref[0].astype(F32) * freq_ref[...]
    cosf = jnp.cos(ang)
    sinf = jnp.sin(ang) * sign_ref[...]
    fill()

    def mix_ret(h):
        cdec = float(np.exp(RET_CHUNK * RET_LOG_DECAY[h]))
        scores, kv, qx, vs = [], [], [], []
        for c in range(n_ret_chunks):
            rows = pl.ds(c * RET_CHUNK, RET_CHUNK)
            cf = cosf[c * RET_CHUNK:(c + 1) * RET_CHUNK]
            sf = sinf[c * RET_CHUNK:(c + 1) * RET_CHUNK]
            q = pret_ref[h, rows, pl.ds(0, RET_DK)]
            k = pret_ref[h, rows, pl.ds(RET_DK, RET_DK)]
            v = pret_ref[h, rows, pl.ds(2 * RET_DK, RET_DV)].astype(BF16)
            q = (q * cf + pltpu.roll(q, RET_DK // 2, 1) * sf) * (RET_DK ** -0.5)
            k = k * cf + pltpu.roll(k, RET_DK // 2, 1) * sf
            scores.append(_dot_nt(q.astype(BF16), k.astype(BF16)))
            kv.append(_dot_tn((k * zeta_ref[h]).astype(BF16), v))
            qx.append((q * xi_ref[h]).astype(BF16))
            vs.append(v)
        fill()
        state = sret_ref[h]
        rets = []
        for c in range(n_ret_chunks):
            p = (scores[c] * dmask_ref[h]).astype(BF16)
            rets.append(_dot(p, vs[c]) + _dot(qx[c], state.astype(BF16)))
            state = cdec * state + kv[c]
        sret_ref[h] = state
        fill()
        finish_previous_tile()
        for c in range(n_ret_chunks):
            rows = pl.ds(c * RET_CHUNK, RET_CHUNK)
            g = pret_ref[h, rows, pl.ds(2 * RET_DK + RET_DV, RET_DV)]
            mixed_ref[rows, pl.ds(h * RET_DV, RET_DV)] = (_rmsnorm(rets[c]) * (g * _sigmoid(g))).astype(BF16)
        fill()
        release("ret", h)

    l0 = lbl_ref[0:1, :]
    l1 = lbl_ref[1:2, :]
    lmax = jnp.maximum(l0, l1)
    e0 = jnp.exp(l0 - lmax)
    lb_all = e0 / (e0 + jnp.exp(l1 - lmax))
    row = lax.broadcasted_iota(jnp.int32, (HG_CHUNK, HG_CHUNK), 0)
    col = lax.broadcasted_iota(jnp.int32, (HG_CHUNK, HG_CHUNK), 1)
    causal = row >= col

    def intra_chunk_scores_safe(q, k, bc):
        trow = lax.broadcasted_iota(jnp.int32, (HG_CHUNK, HG_DK), 0)
        xor = jnp.bitwise_xor(row, col)
        acc = jnp.where(row == col, jnp.sum(q * k, axis=-1, keepdims=True), 0.0)
        half = 1
        while half < HG_CHUNK:
            block = 2 * half
            upper = jnp.bitwise_and(trow, half) != 0
            if block >= 8:
                b3 = bc.reshape(HG_CHUNK // block, block, HG_DK)
                bm = jnp.broadcast_to(b3[:, half - 1:half, :], b3.shape).reshape(HG_CHUNK, HG_DK)
            elif half == 2:
                b3 = bc.reshape(HG_CHUNK // 8, 8, HG_DK)
                sub = lax.broadcasted_iota(jnp.int32, b3.shape, 1)
                bm = jnp.where(sub < 4, jnp.broadcast_to(b3[:, 1:2, :], b3.shape),
                               jnp.broadcast_to(b3[:, 5:6, :], b3.shape)).reshape(HG_CHUNK, HG_DK)
            else:
                bm = jnp.where(upper, pltpu.roll(bc, 1, 0), bc)
            u = (jnp.where(upper, q, k) * jnp.exp(-jnp.abs(bc - bm))).astype(BF16)
            level = (row > col) & (xor >= half) & (xor < block)
            acc = jnp.where(level, _dot_nt(u, u), acc)
            half = block
        return acc

    sublane = jnp.bitwise_and(lax.broadcasted_iota(jnp.int32, (tm, HG_DK), 0), 7)

    def chunk_cumsum(x):
        s8 = x
        for k in (1, 2, 4):
            s8 = s8 + jnp.where(sublane >= k, pltpu.roll(s8, k, 0), 0.0)
        blocks = []
        for c in range(n_hg_chunks):
            carry = None
            for j in range(HG_CHUNK // 8):
                r0 = c * HG_CHUNK + 8 * j
                blk = s8[r0:r0 + 8] if carry is None else s8[r0:r0 + 8] + carry
                blocks.append(blk)
                carry = jnp.broadcast_to(blk[7:8], blk.shape)
        return jnp.concatenate(blocks, axis=0)

    def mix_hg(h):
        lb = lb_all[:, h * HG_DK:(h + 1) * HG_DK]
        fp = phg_ref[h, :, pl.ds(HG_DK, HG_DK)]
        sg = _sigmoid(fp)
        logf = jnp.log(lb + (1.0 - lb) * sg)
        kk = (1.0 - lb) * (1.0 - sg)
        b = chunk_cumsum(logf)
        fill()
        hq = phg_ref[h, :, pl.ds(0, HG_DK)]
        qd = (hq * jnp.exp(b)).astype(BF16)
        if not safe:
            kd = (kk * jnp.exp(-b)).astype(BF16)
        attn, kvt, elast, vs = [], [], [], []
        for c in range(n_hg_chunks):
            r0, r1 = c * HG_CHUNK, (c + 1) * HG_CHUNK
            b_last = b[r1 - 1:r1, :]
            ks = (kk[r0:r1] * jnp.exp(b_last - b[r0:r1])).astype(BF16)
            v = phg_ref[h, pl.ds(r0, HG_CHUNK), pl.ds(2 * HG_DK, HG_DV)].astype(BF16)
            if safe:
                attn.append(intra_chunk_scores_safe(hq[r0:r1], kk[r0:r1], b[r0:r1]))
            else:
                attn.append(jnp.where(causal, _dot_nt(qd[r0:r1], kd[r0:r1]), 0.0))
            kvt.append(_dot_tn(v, ks))
            elast.append(jnp.exp(b_last))
            vs.append(v)
        fill()
        state_t = shg_ref[h]
        outs = []
        for c in range(n_hg_chunks):
            r0, r1 = c * HG_CHUNK, (c + 1) * HG_CHUNK
            outs.append(_dot(attn[c].astype(BF16), vs[c]) + _dot_nt(qd[r0:r1], state_t.astype(BF16)))
            state_t = state_t * elast[c] + kvt[c]
        shg_ref[h] = state_t
        fill()
        finish_previous_tile()
        hgw = hgw_ref[:, pl.ds(h * HG_DV, HG_DV)]
        for c in range(n_hg_chunks):
            rows = pl.ds(c * HG_CHUNK, HG_CHUNK)
            g = phg_ref[h, rows, pl.ds(2 * HG_DK + HG_DV, HG_DV)]
            mixed_ref[rows, pl.ds(RET_WIDTH + h * HG_DV, HG_DV)] = (
                _rmsnorm(outs[c]) * hgw * (g * _sigmoid(g))).astype(BF16)
        release("hg", h)

    for kind, h in HEAD_ORDER:
        if kind == "ret":
            mix_ret(h)
        else:
            mix_hg(h)
    fill(len(pending))
    if deferred_pair is not None:
        hold_ref[...] = h_ref[...]


def _needs_safe_path(lb_logits):
    lg = lb_logits.astype(F32)
    neg_log_lb = -jax.nn.log_softmax(lg, axis=0)[0]
    return HG_CHUNK * jnp.max(neg_log_lb) > MAX_FAST_EXPONENT


def _ret_constants():
    lg = np.asarray(RET_LOG_DECAY, np.float64)
    idx = np.arange(RET_CHUNK, dtype=np.float64)
    diff = idx[:, None] - idx[None, :]
    dmask = np.where(diff >= 0, np.exp(np.where(diff >= 0, diff, 0.0)[None] * lg[:, None, None]), 0.0)
    zeta = np.exp((RET_CHUNK - 1 - idx)[None, :] * lg[:, None])
    xi = np.exp((idx + 1)[None, :] * lg[:, None])
    bc = lambda a: np.broadcast_to(a[:, :, None], (RET_HEADS, RET_CHUNK, RET_DK))
    return jnp.asarray(dmask, F32), jnp.asarray(bc(zeta), F32), jnp.asarray(bc(xi), F32)


def _fused_call(x2d, pos3, mod3, w_in_slabs, w_out_slabs, ln_w, ln_b, lb_logits, hg_norm_w, *, seq, safe):
    m, d = x2d.shape
    tm = min(SAFE_TOKEN_TILE if safe else TOKEN_TILE, seq)
    steps_per_batch = seq // tm
    n_tiles = m // tm
    dmask, zeta, xi = _ret_constants()
    half = RET_DK // 2
    freqs = ROPE_BASE ** (-jnp.arange(half, dtype=F32) / half)
    freq2 = jnp.concatenate([freqs, freqs]).reshape(1, RET_DK)
    sign = jnp.concatenate([-jnp.ones((half,), F32), jnp.ones((half,), F32)]).reshape(1, RET_DK)

    last = n_tiles - 1
    cur = lambda s: jnp.minimum(s, last)
    mixd = lambda s: jnp.clip(s - 1, 0, last)
    done = lambda s: jnp.maximum(s - 2, 0)
    resident = lambda a: pl.BlockSpec(a.shape, lambda s: (0,) * a.ndim, pipeline_mode=pl.Buffered(1))
    return pl.pallas_call(
        functools.partial(_fused_kernel, tm=tm, steps_per_batch=steps_per_batch, n_tiles=n_tiles, safe=safe),
        grid=(n_tiles + 2,),
        in_specs=[pl.BlockSpec(memory_space=pl.ANY),
                  pl.BlockSpec((1, tm, 1), lambda s: (mixd(s) // steps_per_batch, mixd(s) % steps_per_batch, 0)),
                  pl.BlockSpec((1, 1, d), lambda s: (cur(s) // steps_per_batch, 0, 0)),
                  pl.BlockSpec((1, 1, d), lambda s: (cur(s) // steps_per_batch, 0, 1)),
                  pl.BlockSpec((1, 1, d), lambda s: (done(s) // steps_per_batch, 0, 2)),
                  resident(w_in_slabs), resident(w_out_slabs), resident(ln_w), resident(ln_b),
                  resident(lb_logits), resident(hg_norm_w), resident(dmask), resident(zeta),
                  resident(xi), resident(freq2), resident(sign)],
        out_specs=pl.BlockSpec((tm, d), lambda s: (done(s), 0), pipeline_mode=pl.Buffered(1)),
        out_shape=jax.ShapeDtypeStruct((m, d), F32),
        scratch_shapes=[pltpu.VMEM((RET_HEADS, tm, RET_GROUP), F32),
                        pltpu.VMEM((HG_HEADS, tm, HG_GROUP), F32),
                        pltpu.VMEM((RET_HEADS, RET_DK, RET_DV), F32),
                        pltpu.VMEM((HG_HEADS, HG_DV, HG_DK), F32),
                        pltpu.VMEM((tm, MIX_WIDTH), BF16),
                        pltpu.VMEM((tm, d), F32),
                        pltpu.VMEM((tm, d), F32),
                        pltpu.VMEM((tm, d), BF16),
                        pltpu.VMEM((16, 128) if safe else (tm, d), BF16),
                        pltpu.SemaphoreType.DMA(()),
                        pltpu.SemaphoreType.DMA(())],
        compiler_params=pltpu.CompilerParams(dimension_semantics=("arbitrary",),
                                             vmem_limit_bytes=VMEM_LIMIT_BYTES),
        name="fused_layer_safe" if safe else "fused_layer",
    )(x2d, pos3, mod3, mod3, mod3, w_in_slabs, w_out_slabs, ln_w, ln_b,
      lb_logits, hg_norm_w, dmask, zeta, xi, freq2, sign)


def kernel(x, c, positions, w_ada, b_ada, w_in, lb_logits, hg_norm_w, w_out, ln_w, ln_b):
    bsz, seq, d = x.shape
    assert w_ada.shape[0] == DEPTH and seq % RET_CHUNK == 0 and d % PIECE == 0
    mod = _ada_call(c, w_ada[0], b_ada[0])
    mod3 = mod.reshape(bsz, 1, 3 * d)
    operands = (x.reshape(bsz * seq, d), positions.reshape(bsz, seq, 1), mod3, w_in[0], w_out[0],
                ln_w[0].reshape(1, d), ln_b[0].reshape(1, d),
                lb_logits.astype(F32), hg_norm_w[0].reshape(1, HG_WIDTH))

    def run(safe, x2d, pos3, mod3_, w_in_f32, w_out_f32, *rest):
        return _fused_call(x2d, pos3, mod3_, _column_slabs(w_in_f32), _column_slabs(w_out_f32),
                           *rest, seq=seq, safe=safe)

    out = lax.cond(_needs_safe_path(lb_logits),
                   lambda ops: run(True, *ops), lambda ops: run(False, *ops), operands)
    return out.reshape(bsz, seq, d)
```

```python
import functools

import numpy as np
import jax
import jax.numpy as jnp
from jax import lax
from jax.experimental import pallas as pl
from jax.experimental.pallas import tpu as pltpu

RET_HEADS = 4
RET_DK = 128
RET_DV = 256
HG_HEADS = 8
HG_DK = 128
HG_DV = 128
RET_CHUNK = 128
HG_CHUNK = 64
ROPE_BASE = 10000.0
EPS = 1e-6
DEPTH = 1
ALPHA = (2.0 * DEPTH) ** 0.25
RET_LOG2_DECAY = tuple(5.0 + 7.0 * h / (RET_HEADS - 1) for h in range(RET_HEADS))
RET_LOG_DECAY = tuple(float(np.log1p(-np.exp2(-e))) for e in RET_LOG2_DECAY)

RET_QK = RET_HEADS * RET_DK
RET_WIDTH = RET_HEADS * RET_DV
HG_QK = HG_HEADS * HG_DK
HG_WIDTH = HG_HEADS * HG_DV
MIX_WIDTH = RET_WIDTH + HG_WIDTH
OFF_RQ = 0
OFF_RK = OFF_RQ + RET_QK
OFF_RV = OFF_RK + RET_QK
OFF_RG = OFF_RV + RET_WIDTH
OFF_HQ = OFF_RG + RET_WIDTH
OFF_HF = OFF_HQ + HG_QK
OFF_HI = OFF_HF + HG_QK
OFF_HG = OFF_HI + HG_WIDTH
IN_WIDTH = OFF_HG + HG_WIDTH
RET_GROUP = 2 * RET_DK + 2 * RET_DV
HG_GROUP = 2 * HG_DK + 2 * HG_DV

F32 = jnp.float32
BF16 = jnp.bfloat16
TOKEN_TILE = 256
SAFE_TOKEN_TILE = 128
PIECE = 256
V7X_VMEM_BYTES = 64 * 1024 * 1024
VMEM_LIMIT_BYTES = V7X_VMEM_BYTES - 512 * 1024
HEAD_ORDER = (("ret", 0), ("ret", 1), ("hg", 0), ("hg", 1), ("hg", 2), ("hg", 3),
              ("ret", 2), ("ret", 3), ("hg", 4), ("hg", 5), ("hg", 6), ("hg", 7))
MAX_FAST_EXPONENT = 80.0
LN_ROWS = 16
LN_BLOCKS_PER_PIECE = 8


def _slab_kernel(w_ref, o_ref):
    for j in range(o_ref.shape[0]):
        o_ref[j] = w_ref[:, pl.ds(j * PIECE, PIECE)].astype(BF16)


def _column_slabs(w):
    k, n = w.shape
    per_step = 2 if (n // PIECE) % 2 == 0 else 1
    return pl.pallas_call(
        _slab_kernel,
        grid=(n // (PIECE * per_step),),
        in_specs=[pl.BlockSpec((k, PIECE * per_step), lambda j: (0, j))],
        out_specs=pl.BlockSpec((per_step, k, PIECE), lambda j: (j, 0, 0)),
        out_shape=jax.ShapeDtypeStruct((n // PIECE, k, PIECE), BF16),
        compiler_params=pltpu.CompilerParams(dimension_semantics=("arbitrary",)),
        name="weight_slabs",
    )(w)


def _sigmoid(x):
    return 1.0 / (1.0 + jnp.exp(-x))


def _ada_kernel(c_ref, w_ref, b_ref, o_ref):
    c = c_ref[...]
    cond = c * _sigmoid(c)
    o_ref[...] = jnp.dot(cond, w_ref[...], preferred_element_type=F32) + b_ref[...]


def _ada_call(c, w, b):
    bsz, d = c.shape
    n = w.shape[1]
    tn = 1024 if n % 1024 == 0 else n
    return pl.pallas_call(
        _ada_kernel,
        grid=(n // tn,),
        in_specs=[pl.BlockSpec((bsz, d), lambda j: (0, 0)),
                  pl.BlockSpec((d, tn), lambda j: (0, j)),
                  pl.BlockSpec((1, tn), lambda j: (0, j))],
        out_specs=pl.BlockSpec((bsz, tn), lambda j: (0, j)),
        out_shape=jax.ShapeDtypeStruct((bsz, n), F32),
        compiler_params=pltpu.CompilerParams(dimension_semantics=("arbitrary",)),
        name="ada_mod",
    )(c, w, b.reshape(1, n))


def _dot(a, b):
    return jnp.dot(a, b, preferred_element_type=F32)


def _dot_nt(a, b):
    return lax.dot_general(a, b, (((1,), (1,)), ((), ())), preferred_element_type=F32)


def _dot_tn(a, b):
    return lax.dot_general(a, b, (((0,), (0,)), ((), ())), preferred_element_type=F32)


def _layernorm(x):
    mu = jnp.mean(x, axis=-1, keepdims=True)
    xc = x - mu
    var = jnp.mean(xc * xc, axis=-1, keepdims=True)
    return xc * lax.rsqrt(var + EPS)


def _rmsnorm(x):
    return x * lax.rsqrt(jnp.mean(x * x, axis=-1, keepdims=True) + EPS)


def _fused_kernel(xhbm_ref, pos_ref, shift_ref, scale_ref, gate_ref, win_ref, wout_ref, lnw_ref,
                  lnb_ref, lbl_ref, hgw_ref, dmask_ref, zeta_ref, xi_ref, freq_ref, sign_ref,
                  o_ref, pret_ref, phg_ref, sret_ref, shg_ref, mixed_ref, xin_ref, xres_ref, h_ref,
                  hold_ref, in_sem, res_sem, *, tm, steps_per_batch, n_tiles, safe):
    s = pl.program_id(0)
    n_ret_chunks = tm // RET_CHUNK
    n_hg_chunks = tm // HG_CHUNK

    def residual_copy(step):
        tile = jnp.maximum(step - 2, 0)
        return pltpu.make_async_copy(xhbm_ref.at[pl.ds(tile * tm, tm), :], xres_ref, res_sem)

    def input_copy(step):
        tile = jnp.minimum(step, n_tiles - 1)
        return pltpu.make_async_copy(xhbm_ref.at[pl.ds(tile * tm, tm), :], xin_ref, in_sem)

    @pl.when(s == 0)
    def _():
        pret_ref[...] = jnp.zeros_like(pret_ref)
        phg_ref[...] = jnp.zeros_like(phg_ref)
        mixed_ref[...] = jnp.zeros_like(mixed_ref)
        hold_ref[...] = jnp.zeros_like(hold_ref)
        residual_copy(s).start()
        input_copy(s).start()

    @pl.when(lax.rem(jnp.maximum(s - 1, 0), steps_per_batch) == 0)
    def _():
        sret_ref[...] = jnp.zeros_like(sret_ref)
        shg_ref[...] = jnp.zeros_like(shg_ref)

    residual_copy(s).wait()
    input_copy(s).wait()

    prev_q = []
    pending = []

    def fill(n=1):
        for _ in range(n):
            while prev_q or pending:
                is_matmul, thunk = (prev_q or pending).pop(0)
                thunk()
                if is_matmul:
                    break

    def finish_previous_tile():
        while prev_q:
            fill()

    mixed_heads = set()
    deferred_pair = None if safe else (HEAD_ORDER[-1][0], HEAD_ORDER[-1][1] & ~1)

    def release(kind, h):
        mixed_heads.add((kind, h))
        if (kind, h ^ 1) in mixed_heads and (kind, h & ~1) != deferred_pair:
            pending.extend((True, p) for p in proj_pieces(kind, h & ~1, h_ref))

    def proj_pieces(kind, h0, lhs_ref):
        dst_ref = pret_ref if kind == "ret" else phg_ref

        def piece(src_col, dests):
            def run():
                r = _dot(lhs_ref[...], win_ref[src_col // PIECE])
                c = 0
                for head, col, width in dests:
                    dst_ref[head, :, pl.ds(col, width)] = r[:, c:c + width]
                    c += width
            return run

        if kind == "ret":
            pieces = [piece(off + h0 * RET_DK, [(h0, col, RET_DK), (h0 + 1, col, RET_DK)])
                      for off, col in ((OFF_RQ, 0), (OFF_RK, RET_DK))]
            pieces += [piece(off + h * RET_DV, [(h, col, RET_DV)])
                       for off, col in ((OFF_RV, 2 * RET_DK), (OFF_RG, 2 * RET_DK + RET_DV))
                       for h in (h0, h0 + 1)]
            return pieces
        return [piece(off + h0 * HG_DK, [(h0, col, HG_DK), (h0 + 1, col, HG_DK)])
                for off, col in ((OFF_HQ, 0), (OFF_HF, HG_DK), (OFF_HI, 2 * HG_DK), (OFF_HG, 2 * HG_DK + HG_DV))]

    def outproj_piece(c0):
        def run():
            o_ref[:, pl.ds(c0, PIECE)] = _dot(mixed_ref[...], wout_ref[c0 // PIECE])
        return run

    def final_layernorm():
        for r0 in range(0, tm, LN_ROWS):
            rows = pl.ds(r0, LN_ROWS)
            z = ALPHA * xres_ref[rows, :] + gate_ref[0] * o_ref[rows, :]
            o_ref[rows, :] = _layernorm(z) * lnw_ref[...] + lnb_ref[...]

        @pl.when(s + 1 < pl.num_programs(0))
        def _():
            residual_copy(s + 1).start()

    prev_q.extend((True, outproj_piece(c0)) for c0 in range(0, o_ref.shape[1], PIECE))
    prev_q.append((False, final_layernorm))
    if deferred_pair is not None:
        pending.extend((True, p) for p in proj_pieces(*deferred_pair, hold_ref))

    for i, r0 in enumerate(range(0, tm, LN_ROWS)):
        rows = pl.ds(r0, LN_ROWS)
        h_ref[rows, :] = (_layernorm(xin_ref[rows, :]) * (1.0 + scale_ref[0]) + shift_ref[0]).astype(BF16)
        if i % LN_BLOCKS_PER_PIECE == LN_BLOCKS_PER_PIECE - 1:
            fill()

    @pl.when(s + 1 < pl.num_programs(0))
    def _():
        input_copy(s + 1).start()

    ang = pos_ref[0].astype(F32) * freq_ref[...]
    cosf = jnp.cos(ang)
    sinf = jnp.sin(ang) * sign_ref[...]
    fill()

    def mix_ret(h):
        cdec = float(np.exp(RET_CHUNK * RET_LOG_DECAY[h]))
        scores, kv, qx, vs = [], [], [], []
        for c in range(n_ret_chunks):
            rows = pl.ds(c * RET_CHUNK, RET_CHUNK)
            cf = cosf[c * RET_CHUNK:(c + 1) * RET_CHUNK]
            sf = sinf[c * RET_CHUNK:(c + 1) * RET_CHUNK]
            q = pret_ref[h, rows, pl.ds(0, RET_DK)]
            k = pret_ref[h, rows, pl.ds(RET_DK, RET_DK)]
            v = pret_ref[h, rows, pl.ds(2 * RET_DK, RET_DV)].astype(BF16)
            q = (q * cf + pltpu.roll(q, RET_DK // 2, 1) * sf) * (RET_DK ** -0.5)
            k = k * cf + pltpu.roll(k, RET_DK // 2, 1) * sf
            scores.append(_dot_nt(q.astype(BF16), k.astype(BF16)))
            kv.append(_dot_tn((k * zeta_ref[h]).astype(BF16), v))
            qx.append((q * xi_ref[h]).astype(BF16))
            vs.append(v)
        fill()
        state = sret_ref[h]
        rets = []
        for c in range(n_ret_chunks):
            p = (scores[c] * dmask_ref[h]).astype(BF16)
            rets.append(_dot(p, vs[c]) + _dot(qx[c], state.astype(BF16)))
            state = cdec * state + kv[c]
        sret_ref[h] = state
        fill()
        finish_previous_tile()
        for c in range(n_ret_chunks):
            rows = pl.ds(c * RET_CHUNK, RET_CHUNK)
            g = pret_ref[h, rows, pl.ds(2 * RET_DK + RET_DV, RET_DV)]
            mixed_ref[rows, pl.ds(h * RET_DV, RET_DV)] = (_rmsnorm(rets[c]) * (g * _sigmoid(g))).astype(BF16)
        fill()
        release("ret", h)

    l0 = lbl_ref[0:1, :]
    l1 = lbl_ref[1:2, :]
    lmax = jnp.maximum(l0, l1)
    e0 = jnp.exp(l0 - lmax)
    lb_all = e0 / (e0 + jnp.exp(l1 - lmax))
    row = lax.broadcasted_iota(jnp.int32, (HG_CHUNK, HG_CHUNK), 0)
    col = lax.broadcasted_iota(jnp.int32, (HG_CHUNK, HG_CHUNK), 1)
    causal = row >= col

    def intra_chunk_scores_safe(q, k, bc):
        trow = lax.broadcasted_iota(jnp.int32, (HG_CHUNK, HG_DK), 0)
        xor = jnp.bitwise_xor(row, col)
        acc = jnp.where(row == col, jnp.sum(q * k, axis=-1, keepdims=True), 0.0)
        half = 1
        while half < HG_CHUNK:
            block = 2 * half
            upper = jnp.bitwise_and(trow, half) != 0
            if block >= 8:
                b3 = bc.reshape(HG_CHUNK // block, block, HG_DK)
                bm = jnp.broadcast_to(b3[:, half - 1:half, :], b3.shape).reshape(HG_CHUNK, HG_DK)
            elif half == 2:
                b3 = bc.reshape(HG_CHUNK // 8, 8, HG_DK)
                sub = lax.broadcasted_iota(jnp.int32, b3.shape, 1)
                bm = jnp.where(sub < 4, jnp.broadcast_to(b3[:, 1:2, :], b3.shape),
                               jnp.broadcast_to(b3[:, 5:6, :], b3.shape)).reshape(HG_CHUNK, HG_DK)
            else:
                bm = jnp.where(upper, pltpu.roll(bc, 1, 0), bc)
            u = (jnp.where(upper, q, k) * jnp.exp(-jnp.abs(bc - bm))).astype(BF16)
            level = (row > col) & (xor >= half) & (xor < block)
            acc = jnp.where(level, _dot_nt(u, u), acc)
            half = block
        return acc

    sublane = jnp.bitwise_and(lax.broadcasted_iota(jnp.int32, (tm, HG_DK), 0), 7)

    def chunk_cumsum(x):
        s8 = x
        for k in (1, 2, 4):
            s8 = s8 + jnp.where(sublane >= k, pltpu.roll(s8, k, 0), 0.0)
        blocks = []
        for c in range(n_hg_chunks):
            carry = None
            for j in range(HG_CHUNK // 8):
                r0 = c * HG_CHUNK + 8 * j
                blk = s8[r0:r0 + 8] if carry is None else s8[r0:r0 + 8] + carry
                blocks.append(blk)
                carry = jnp.broadcast_to(blk[7:8], blk.shape)
        return jnp.concatenate(blocks, axis=0)

    def mix_hg(h):
        lb = lb_all[:, h * HG_DK:(h + 1) * HG_DK]
        fp = phg_ref[h, :, pl.ds(HG_DK, HG_DK)]
        sg = _sigmoid(fp)
        logf = jnp.log(lb + (1.0 - lb) * sg)
        kk = (1.0 - lb) * (1.0 - sg)
        b = chunk_cumsum(logf)
        fill()
        hq = phg_ref[h, :, pl.ds(0, HG_DK)]
        qd = (hq * jnp.exp(b)).astype(BF16)
        if not safe:
            kd = (kk * jnp.exp(-b)).astype(BF16)
        attn, kvt, elast, vs = [], [], [], []
        for c in range(n_hg_chunks):
            r0, r1 = c * HG_CHUNK, (c + 1) * HG_CHUNK
            b_last = b[r1 - 1:r1, :]
            ks = (kk[r0:r1] * jnp.exp(b_last - b[r0:r1])).astype(BF16)
            v = phg_ref[h, pl.ds(r0, HG_CHUNK), pl.ds(2 * HG_DK, HG_DV)].astype(BF16)
            if safe:
                attn.append(intra_chunk_scores_safe(hq[r0:r1], kk[r0:r1], b[r0:r1]))
            else:
                attn.append(jnp.where(causal, _dot_nt(qd[r0:r1], kd[r0:r1]), 0.0))
            kvt.append(_dot_tn(v, ks))
            elast.append(jnp.exp(b_last))
            vs.append(v)
        fill()
        state_t = shg_ref[h]
        outs = []
        for c in range(n_hg_chunks):
            r0, r1 = c * HG_CHUNK, (c + 1) * HG_CHUNK
            outs.append(_dot(attn[c].astype(BF16), vs[c]) + _dot_nt(qd[r0:r1], state_t.astype(BF16)))
            state_t = state_t * elast[c] + kvt[c]
        shg_ref[h] = state_t
        fill()
        finish_previous_tile()
        hgw = hgw_ref[:, pl.ds(h * HG_DV, HG_DV)]
        for c in range(n_hg_chunks):
            rows = pl.ds(c * HG_CHUNK, HG_CHUNK)
            g = phg_ref[h, rows, pl.ds(2 * HG_DK + HG_DV, HG_DV)]
            mixed_ref[rows, pl.ds(RET_WIDTH + h * HG_DV, HG_DV)] = (
                _rmsnorm(outs[c]) * hgw * (g * _sigmoid(g))).astype(BF16)
        release("hg", h)

    for kind, h in HEAD_ORDER:
        if kind == "ret":
            mix_ret(h)
        else:
            mix_hg(h)
    fill(len(pending))
    if deferred_pair is not None:
        hold_ref[...] = h_ref[...]


def _needs_safe_path(lb_logits):
    lg = lb_logits.astype(F32)
    neg_log_lb = -jax.nn.log_softmax(lg, axis=0)[0]
    return HG_CHUNK * jnp.max(neg_log_lb) > MAX_FAST_EXPONENT


def _ret_constants():
    lg = np.asarray(RET_LOG_DECAY, np.float64)
    idx = np.arange(RET_CHUNK, dtype=np.float64)
    diff = idx[:, None] - idx[None, :]
    dmask = np.where(diff >= 0, np.exp(np.where(diff >= 0, diff, 0.0)[None] * lg[:, None, None]), 0.0)
    zeta = np.exp((RET_CHUNK - 1 - idx)[None, :] * lg[:, None])
    xi = np.exp((idx + 1)[None, :] * lg[:, None])
    bc = lambda a: np.broadcast_to(a[:, :, None], (RET_HEADS, RET_CHUNK, RET_DK))
    return jnp.asarray(dmask, F32), jnp.asarray(bc(zeta), F32), jnp.asarray(bc(xi), F32)


def _fused_call(x2d, pos3, mod3, w_in_slabs, w_out_slabs, ln_w, ln_b, lb_logits, hg_norm_w, *, seq, safe):
    m, d = x2d.shape
    tm = min(SAFE_TOKEN_TILE if safe else TOKEN_TILE, seq)
    steps_per_batch = seq // tm
    n_tiles = m // tm
    dmask, zeta, xi = _ret_constants()
    half = RET_DK // 2
    freqs = ROPE_BASE ** (-jnp.arange(half, dtype=F32) / half)
    freq2 = jnp.concatenate([freqs, freqs]).reshape(1, RET_DK)
    sign = jnp.concatenate([-jnp.ones((half,), F32), jnp.ones((half,), F32)]).reshape(1, RET_DK)

    last = n_tiles - 1
    cur = lambda s: jnp.minimum(s, last)
    mixd = lambda s: jnp.clip(s - 1, 0, last)
    done = lambda s: jnp.maximum(s - 2, 0)
    resident = lambda a: pl.BlockSpec(a.shape, lambda s: (0,) * a.ndim, pipeline_mode=pl.Buffered(1))
    return pl.pallas_call(
        functools.partial(_fused_kernel, tm=tm, steps_per_batch=steps_per_batch, n_tiles=n_tiles, safe=safe),
        grid=(n_tiles + 2,),
        in_specs=[pl.BlockSpec(memory_space=pl.ANY),
                  pl.BlockSpec((1, tm, 1), lambda s: (mixd(s) // steps_per_batch, mixd(s) % steps_per_batch, 0)),
                  pl.BlockSpec((1, 1, d), lambda s: (cur(s) // steps_per_batch, 0, 0)),
                  pl.BlockSpec((1, 1, d), lambda s: (cur(s) // steps_per_batch, 0, 1)),
                  pl.BlockSpec((1, 1, d), lambda s: (done(s) // steps_per_batch, 0, 2)),
                  resident(w_in_slabs), resident(w_out_slabs), resident(ln_w), resident(ln_b),
                  resident(lb_logits), resident(hg_norm_w), resident(dmask), resident(zeta),
                  resident(xi), resident(freq2), resident(sign)],
        out_specs=pl.BlockSpec((tm, d), lambda s: (done(s), 0), pipeline_mode=pl.Buffered(1)),
        out_shape=jax.ShapeDtypeStruct((m, d), F32),
        scratch_shapes=[pltpu.VMEM((RET_HEADS, tm, RET_GROUP), F32),
                        pltpu.VMEM((HG_HEADS, tm, HG_GROUP), F32),
                        pltpu.VMEM((RET_HEADS, RET_DK, RET_DV), F32),
                        pltpu.VMEM((HG_HEADS, HG_DV, HG_DK), F32),
                        pltpu.VMEM((tm, MIX_WIDTH), BF16),
                        pltpu.VMEM((tm, d), F32),
                        pltpu.VMEM((tm, d), F32),
                        pltpu.VMEM((tm, d), BF16),
                        pltpu.VMEM((16, 128) if safe else (tm, d), BF16),
                        pltpu.SemaphoreType.DMA(()),
                        pltpu.SemaphoreType.DMA(())],
        compiler_params=pltpu.CompilerParams(dimension_semantics=("arbitrary",),
                                             vmem_limit_bytes=VMEM_LIMIT_BYTES),
        name="fused_layer_safe" if safe else "fused_layer",
    )(x2d, pos3, mod3, mod3, mod3, w_in_slabs, w_out_slabs, ln_w, ln_b,
      lb_logits, hg_norm_w, dmask, zeta, xi, freq2, sign)


def kernel(x, c, positions, w_ada, b_ada, w_in, lb_logits, hg_norm_w, w_out, ln_w, ln_b):
    bsz, seq, d = x.shape
    assert w_ada.shape[0] == DEPTH and seq % RET_CHUNK == 0 and d % PIECE == 0
    mod = _ada_call(c, w_ada[0], b_ada[0])
    mod3 = mod.reshape(bsz, 1, 3 * d)
    operands = (x.reshape(bsz * seq, d), positions.reshape(bsz, seq, 1), mod3, w_in[0], w_out[0],
                ln_w[0].reshape(1, d), ln_b[0].reshape(1, d),
                lb_logits.astype(F32), hg_norm_w[0].reshape(1, HG_WIDTH))

    def run(safe, x2d, pos3, mod3_, w_in_f32, w_out_f32, *rest):
        return _fused_call(x2d, pos3, mod3_, _column_slabs(w_in_f32), _column_slabs(w_out_f32),
                           *rest, seq=seq, safe=safe)

    out = lax.cond(_needs_safe_path(lb_logits),
                   lambda ops: run(True, *ops), lambda ops: run(False, *ops), operands)
    return out.reshape(bsz, seq, d)
```

```python
import functools

import numpy as np
import jax
import jax.numpy as jnp
from jax import lax
from jax.experimental import pallas as pl
from jax.experimental.pallas import tpu as pltpu

RET_HEADS = 4
RET_DK = 128
RET_DV = 256
HG_HEADS = 8
HG_DK = 128
HG_DV = 128
RET_CHUNK = 128
HG_CHUNK = 64
ROPE_BASE = 10000.0
EPS = 1e-6
DEPTH = 1
ALPHA = (2.0 * DEPTH) ** 0.25
RET_LOG2_DECAY = tuple(5.0 + 7.0 * h / (RET_HEADS - 1) for h in range(RET_HEADS))
RET_LOG_DECAY = tuple(float(np.log1p(-np.exp2(-e))) for e in RET_LOG2_DECAY)

RET_QK = RET_HEADS * RET_DK
RET_WIDTH = RET_HEADS * RET_DV
HG_QK = HG_HEADS * HG_DK
HG_WIDTH = HG_HEADS * HG_DV
MIX_WIDTH = RET_WIDTH + HG_WIDTH
OFF_RQ = 0
OFF_RK = OFF_RQ + RET_QK
OFF_RV = OFF_RK + RET_QK
OFF_RG = OFF_RV + RET_WIDTH
OFF_HQ = OFF_RG + RET_WIDTH
OFF_HF = OFF_HQ + HG_QK
OFF_HI = OFF_HF + HG_QK
OFF_HG = OFF_HI + HG_WIDTH
IN_WIDTH = OFF_HG + HG_WIDTH
RET_GROUP = 2 * RET_DK + 2 * RET_DV
HG_GROUP = 2 * HG_DK + 2 * HG_DV

F32 = jnp.float32
BF16 = jnp.bfloat16
TOKEN_TILE = 256
SAFE_TOKEN_TILE = 128
PIECE = 256
V7X_VMEM_BYTES = 64 * 1024 * 1024
VMEM_LIMIT_BYTES = V7X_VMEM_BYTES - 512 * 1024
HEAD_ORDER = (("hg", 0), ("hg", 1), ("ret", 0), ("ret", 1), ("hg", 2), ("hg", 3),
              ("ret", 2), ("ret", 3), ("hg", 4), ("hg", 5), ("hg", 6), ("hg", 7))
MAX_FAST_EXPONENT = 80.0
LN_ROWS = 16
LN_BLOCKS_PER_PIECE = 8


def _slab_kernel(w_ref, o_ref):
    for j in range(o_ref.shape[0]):
        o_ref[j] = w_ref[:, pl.ds(j * PIECE, PIECE)].astype(BF16)


def _column_slabs(w):
    k, n = w.shape
    per_step = 2 if (n // PIECE) % 2 == 0 else 1
    return pl.pallas_call(
        _slab_kernel,
        grid=(n // (PIECE * per_step),),
        in_specs=[pl.BlockSpec((k, PIECE * per_step), lambda j: (0, j))],
        out_specs=pl.BlockSpec((per_step, k, PIECE), lambda j: (j, 0, 0)),
        out_shape=jax.ShapeDtypeStruct((n // PIECE, k, PIECE), BF16),
        compiler_params=pltpu.CompilerParams(dimension_semantics=("arbitrary",)),
        name="weight_slabs",
    )(w)


def _sigmoid(x):
    return 1.0 / (1.0 + jnp.exp(-x))


def _ada_kernel(c_ref, w_ref, b_ref, o_ref):
    c = c_ref[...]
    cond = c * _sigmoid(c)
    o_ref[...] = jnp.dot(cond, w_ref[...], preferred_element_type=F32) + b_ref[...]


def _ada_call(c, w, b):
    bsz, d = c.shape
    n = w.shape[1]
    tn = 1024 if n % 1024 == 0 else n
    return pl.pallas_call(
        _ada_kernel,
        grid=(n // tn,),
        in_specs=[pl.BlockSpec((bsz, d), lambda j: (0, 0)),
                  pl.BlockSpec((d, tn), lambda j: (0, j)),
                  pl.BlockSpec((1, tn), lambda j: (0, j))],
        out_specs=pl.BlockSpec((bsz, tn), lambda j: (0, j)),
        out_shape=jax.ShapeDtypeStruct((bsz, n), F32),
        compiler_params=pltpu.CompilerParams(dimension_semantics=("arbitrary",)),
        name="ada_mod",
    )(c, w, b.reshape(1, n))


def _dot(a, b):
    return jnp.dot(a, b, preferred_element_type=F32)


def _dot_nt(a, b):
    return lax.dot_general(a, b, (((1,), (1,)), ((), ())), preferred_element_type=F32)


def _dot_tn(a, b):
    return lax.dot_general(a, b, (((0,), (0,)), ((), ())), preferred_element_type=F32)


def _layernorm(x):
    mu = jnp.mean(x, axis=-1, keepdims=True)
    xc = x - mu
    var = jnp.mean(xc * xc, axis=-1, keepdims=True)
    return xc * lax.rsqrt(var + EPS)


def _rmsnorm(x):
    return x * lax.rsqrt(jnp.mean(x * x, axis=-1, keepdims=True) + EPS)


def _fused_kernel(xhbm_ref, pos_ref, shift_ref, scale_ref, gate_ref, win_ref, wout_ref, lnw_ref,
                  lnb_ref, lbl_ref, hgw_ref, dmask_ref, zeta_ref, xi_ref, freq_ref, sign_ref,
                  o_ref, pret_ref, phg_ref, sret_ref, shg_ref, mixed_ref, xin_ref, xres_ref, h_ref,
                  hold_ref, in_sem, res_sem, *, tm, steps_per_batch, n_tiles, safe):
    s = pl.program_id(0)
    n_ret_chunks = tm // RET_CHUNK
    n_hg_chunks = tm // HG_CHUNK

    def residual_copy(step):
        tile = jnp.maximum(step - 2, 0)
        return pltpu.make_async_copy(xhbm_ref.at[pl.ds(tile * tm, tm), :], xres_ref, res_sem)

    def input_copy(step):
        tile = jnp.minimum(step, n_tiles - 1)
        return pltpu.make_async_copy(xhbm_ref.at[pl.ds(tile * tm, tm), :], xin_ref, in_sem)

    @pl.when(s == 0)
    def _():
        pret_ref[...] = jnp.zeros_like(pret_ref)
        phg_ref[...] = jnp.zeros_like(phg_ref)
        mixed_ref[...] = jnp.zeros_like(mixed_ref)
        hold_ref[...] = jnp.zeros_like(hold_ref)
        residual_copy(s).start()
        input_copy(s).start()

    @pl.when(lax.rem(jnp.maximum(s - 1, 0), steps_per_batch) == 0)
    def _():
        sret_ref[...] = jnp.zeros_like(sret_ref)
        shg_ref[...] = jnp.zeros_like(shg_ref)

    residual_copy(s).wait()
    input_copy(s).wait()

    prev_q = []
    pending = []

    def fill(n=1):
        for _ in range(n):
            while prev_q or pending:
                is_matmul, thunk = (prev_q or pending).pop(0)
                thunk()
                if is_matmul:
                    break

    def finish_previous_tile():
        while prev_q:
            fill()

    mixed_heads = set()
    deferred_pair = None if safe else (HEAD_ORDER[-1][0], HEAD_ORDER[-1][1] & ~1)

    def release(kind, h):
        mixed_heads.add((kind, h))
        if (kind, h ^ 1) in mixed_heads and (kind, h & ~1) != deferred_pair:
            pending.extend((True, p) for p in proj_pieces(kind, h & ~1, h_ref))

    def proj_pieces(kind, h0, lhs_ref):
        dst_ref = pret_ref if kind == "ret" else phg_ref

        def piece(src_col, dests):
            def run():
                r = _dot(lhs_ref[...], win_ref[src_col // PIECE])
                c = 0
                for head, col, width in dests:
                    dst_ref[head, :, pl.ds(col, width)] = r[:, c:c + width]
                    c += width
            return run

        if kind == "ret":
            pieces = [piece(off + h0 * RET_DK, [(h0, col, RET_DK), (h0 + 1, col, RET_DK)])
                      for off, col in ((OFF_RQ, 0), (OFF_RK, RET_DK))]
            pieces += [piece(off + h * RET_DV, [(h, col, RET_DV)])
                       for off, col in ((OFF_RV, 2 * RET_DK), (OFF_RG, 2 * RET_DK + RET_DV))
                       for h in (h0, h0 + 1)]
            return pieces
        return [piece(off + h0 * HG_DK, [(h0, col, HG_DK), (h0 + 1, col, HG_DK)])
                for off, col in ((OFF_HQ, 0), (OFF_HF, HG_DK), (OFF_HI, 2 * HG_DK), (OFF_HG, 2 * HG_DK + HG_DV))]

    def outproj_piece(c0):
        def run():
            o_ref[:, pl.ds(c0, PIECE)] = _dot(mixed_ref[...], wout_ref[c0 // PIECE])
        return run

    def final_layernorm():
        for r0 in range(0, tm, LN_ROWS):
            rows = pl.ds(r0, LN_ROWS)
            z = ALPHA * xres_ref[rows, :] + gate_ref[0] * o_ref[rows, :]
            o_ref[rows, :] = _layernorm(z) * lnw_ref[...] + lnb_ref[...]

        @pl.when(s + 1 < pl.num_programs(0))
        def _():
            residual_copy(s + 1).start()

    prev_q.extend((True, outproj_piece(c0)) for c0 in range(0, o_ref.shape[1], PIECE))
    prev_q.append((False, final_layernorm))
    if deferred_pair is not None:
        pending.extend((True, p) for p in proj_pieces(*deferred_pair, hold_ref))

    for i, r0 in enumerate(range(0, tm, LN_ROWS)):
        rows = pl.ds(r0, LN_ROWS)
        h_ref[rows, :] = (_layernorm(xin_ref[rows, :]) * (1.0 + scale_ref[0]) + shift_ref[0]).astype(BF16)
        if i % LN_BLOCKS_PER_PIECE == LN_BLOCKS_PER_PIECE - 1:
            fill()

    @pl.when(s + 1 < pl.num_programs(0))
    def _():
        input_copy(s + 1).start()

    ang = pos_ref[0].astype(F32) * freq_ref[...]
    cosf = jnp.cos(ang)
    sinf = jnp.sin(ang) * sign_ref[...]
    fill()

    def mix_ret(h):
        cdec = float(np.exp(RET_CHUNK * RET_LOG_DECAY[h]))
        scores, kv, qx, vs = [], [], [], []
        for c in range(n_ret_chunks):
            rows = pl.ds(c * RET_CHUNK, RET_CHUNK)
            cf = cosf[c * RET_CHUNK:(c + 1) * RET_CHUNK]
            sf = sinf[c * RET_CHUNK:(c + 1) * RET_CHUNK]
            q = pret_ref[h, rows, pl.ds(0, RET_DK)]
            k = pret_ref[h, rows, pl.ds(RET_DK, RET_DK)]
            v = pret_ref[h, rows, pl.ds(2 * RET_DK, RET_DV)].astype(BF16)
            q = (q * cf + pltpu.roll(q, RET_DK // 2, 1) * sf) * (RET_DK ** -0.5)
            k = k * cf + pltpu.roll(k, RET_DK // 2, 1) * sf
            scores.append(_dot_nt(q.astype(BF16), k.astype(BF16)))
            kv.append(_dot_tn((k * zeta_ref[h]).astype(BF16), v))
            qx.append((q * xi_ref[h]).astype(BF16))
            vs.append(v)
        fill()
        state = sret_ref[h]
        rets = []
        for c in range(n_ret_chunks):
            p = (scores[c] * dmask_ref[h]).astype(BF16)
            rets.append(_dot(p, vs[c]) + _dot(qx[c], state.astype(BF16)))
            state = cdec * state + kv[c]
        sret_ref[h] = state
        fill()
        finish_previous_tile()
        for c in range(n_ret_chunks):
            rows = pl.ds(c * RET_CHUNK, RET_CHUNK)
            g = pret_ref[h, rows, pl.ds(2 * RET_DK + RET_DV, RET_DV)]
            mixed_ref[rows, pl.ds(h * RET_DV, RET_DV)] = (_rmsnorm(rets[c]) * (g * _sigmoid(g))).astype(BF16)
        fill()
        release("ret", h)

    l0 = lbl_ref[0:1, :]
    l1 = lbl_ref[1:2, :]
    lmax = jnp.maximum(l0, l1)
    e0 = jnp.exp(l0 - lmax)
    lb_all = e0 / (e0 + jnp.exp(l1 - lmax))
    row = lax.broadcasted_iota(jnp.int32, (HG_CHUNK, HG_CHUNK), 0)
    col = lax.broadcasted_iota(jnp.int32, (HG_CHUNK, HG_CHUNK), 1)
    causal = row >= col

    def intra_chunk_scores_safe(q, k, bc):
        trow = lax.broadcasted_iota(jnp.int32, (HG_CHUNK, HG_DK), 0)
        xor = jnp.bitwise_xor(row, col)
        acc = jnp.where(row == col, jnp.sum(q * k, axis=-1, keepdims=True), 0.0)
        half = 1
        while half < HG_CHUNK:
            block = 2 * half
            upper = jnp.bitwise_and(trow, half) != 0
            if block >= 8:
                b3 = bc.reshape(HG_CHUNK // block, block, HG_DK)
                bm = jnp.broadcast_to(b3[:, half - 1:half, :], b3.shape).reshape(HG_CHUNK, HG_DK)
            elif half == 2:
                b3 = bc.reshape(HG_CHUNK // 8, 8, HG_DK)
                sub = lax.broadcasted_iota(jnp.int32, b3.shape, 1)
                bm = jnp.where(sub < 4, jnp.broadcast_to(b3[:, 1:2, :], b3.shape),
                               jnp.broadcast_to(b3[:, 5:6, :], b3.shape)).reshape(HG_CHUNK, HG_DK)
            else:
                bm = jnp.where(upper, pltpu.roll(bc, 1, 0), bc)
            u = (jnp.where(upper, q, k) * jnp.exp(-jnp.abs(bc - bm))).astype(BF16)
            level = (row > col) & (xor >= half) & (xor < block)
            acc = jnp.where(level, _dot_nt(u, u), acc)
            half = block
        return acc

    sublane = jnp.bitwise_and(lax.broadcasted_iota(jnp.int32, (tm, HG_DK), 0), 7)

    def chunk_cumsum(x):
        s8 = x
        for k in (1, 2, 4):
            s8 = s8 + jnp.where(sublane >= k, pltpu.roll(s8, k, 0), 0.0)
        blocks = []
        for c in range(n_hg_chunks):
            carry = None
            for j in range(HG_CHUNK // 8):
                r0 = c * HG_CHUNK + 8 * j
                blk = s8[r0:r0 + 8] if carry is None else s8[r0:r0 + 8] + carry
                blocks.append(blk)
                carry = jnp.broadcast_to(blk[7:8], blk.shape)
        return jnp.concatenate(blocks, axis=0)

    def mix_hg(h):
        lb = lb_all[:, h * HG_DK:(h + 1) * HG_DK]
        fp = phg_ref[h, :, pl.ds(HG_DK, HG_DK)]
        sg = _sigmoid(fp)
        logf = jnp.log(lb + (1.0 - lb) * sg)
        kk = (1.0 - lb) * (1.0 - sg)
        b = chunk_cumsum(logf)
        fill()
        hq = phg_ref[h, :, pl.ds(0, HG_DK)]
        qd = (hq * jnp.exp(b)).astype(BF16)
        if not safe:
            kd = (kk * jnp.exp(-b)).astype(BF16)
        attn, kvt, elast, vs = [], [], [], []
        for c in range(n_hg_chunks):
            r0, r1 = c * HG_CHUNK, (c + 1) * HG_CHUNK
            b_last = b[r1 - 1:r1, :]
            ks = (kk[r0:r1] * jnp.exp(b_last - b[r0:r1])).astype(BF16)
            v = phg_ref[h, pl.ds(r0, HG_CHUNK), pl.ds(2 * HG_DK, HG_DV)].astype(BF16)
            if safe:
                attn.append(intra_chunk_scores_safe(hq[r0:r1], kk[r0:r1], b[r0:r1]))
            else:
                attn.append(jnp.where(causal, _dot_nt(qd[r0:r1], kd[r0:r1]), 0.0))
            kvt.append(_dot_tn(v, ks))
            elast.append(jnp.exp(b_last))
            vs.append(v)
        fill()
        state_t = shg_ref[h]
        outs = []
        for c in range(n_hg_chunks):
            r0, r1 = c * HG_CHUNK, (c + 1) * HG_CHUNK
            outs.append(_dot(attn[c].astype(BF16), vs[c]) + _dot_nt(qd[r0:r1], state_t.astype(BF16)))
            state_t = state_t * elast[c] + kvt[c]
        shg_ref[h] = state_t
        fill()
        finish_previous_tile()
        hgw = hgw_ref[:, pl.ds(h * HG_DV, HG_DV)]
        for c in range(n_hg_chunks):
            rows = pl.ds(c * HG_CHUNK, HG_CHUNK)
            g = phg_ref[h, rows, pl.ds(2 * HG_DK + HG_DV, HG_DV)]
            mixed_ref[rows, pl.ds(RET_WIDTH + h * HG_DV, HG_DV)] = (
                _rmsnorm(outs[c]) * hgw * (g * _sigmoid(g))).astype(BF16)
        release("hg", h)

    for kind, h in HEAD_ORDER:
        if kind == "ret":
            mix_ret(h)
        else:
            mix_hg(h)
    fill(len(pending))
    if deferred_pair is not None:
        hold_ref[...] = h_ref[...]


def _needs_safe_path(lb_logits):
    lg = lb_logits.astype(F32)
    neg_log_lb = -jax.nn.log_softmax(lg, axis=0)[0]
    return HG_CHUNK * jnp.max(neg_log_lb) > MAX_FAST_EXPONENT


def _ret_constants():
    lg = np.asarray(RET_LOG_DECAY, np.float64)
    idx = np.arange(RET_CHUNK, dtype=np.float64)
    diff = idx[:, None] - idx[None, :]
    dmask = np.where(diff >= 0, np.exp(np.where(diff >= 0, diff, 0.0)[None] * lg[:, None, None]), 0.0)
    zeta = np.exp((RET_CHUNK - 1 - idx)[None, :] * lg[:, None])
    xi = np.exp((idx + 1)[None, :] * lg[:, None])
    bc = lambda a: np.broadcast_to(a[:, :, None], (RET_HEADS, RET_CHUNK, RET_DK))
    return jnp.asarray(dmask, F32), jnp.asarray(bc(zeta), F32), jnp.asarray(bc(xi), F32)


def _fused_call(x2d, pos3, mod3, w_in_slabs, w_out_slabs, ln_w, ln_b, lb_logits, hg_norm_w, *, seq, safe):
    m, d = x2d.shape
    tm = min(SAFE_TOKEN_TILE if safe else TOKEN_TILE, seq)
    steps_per_batch = seq // tm
    n_tiles = m // tm
    dmask, zeta, xi = _ret_constants()
    half = RET_DK // 2
    freqs = ROPE_BASE ** (-jnp.arange(half, dtype=F32) / half)
    freq2 = jnp.concatenate([freqs, freqs]).reshape(1, RET_DK)
    sign = jnp.concatenate([-jnp.ones((half,), F32), jnp.ones((half,), F32)]).reshape(1, RET_DK)

    last = n_tiles - 1
    cur = lambda s: jnp.minimum(s, last)
    mixd = lambda s: jnp.clip(s - 1, 0, last)
    done = lambda s: jnp.maximum(s - 2, 0)
    resident = lambda a: pl.BlockSpec(a.shape, lambda s: (0,) * a.ndim, pipeline_mode=pl.Buffered(1))
    return pl.pallas_call(
        functools.partial(_fused_kernel, tm=tm, steps_per_batch=steps_per_batch, n_tiles=n_tiles, safe=safe),
        grid=(n_tiles + 2,),
        in_specs=[pl.BlockSpec(memory_space=pl.ANY),
                  pl.BlockSpec((1, tm, 1), lambda s: (mixd(s) // steps_per_batch, mixd(s) % steps_per_batch, 0)),
                  pl.BlockSpec((1, 1, d), lambda s: (cur(s) // steps_per_batch, 0, 0)),
                  pl.BlockSpec((1, 1, d), lambda s: (cur(s) // steps_per_batch, 0, 1)),
                  pl.BlockSpec((1, 1, d), lambda s: (done(s) // steps_per_batch, 0, 2)),
                  resident(w_in_slabs), resident(w_out_slabs), resident(ln_w), resident(ln_b),
                  resident(lb_logits), resident(hg_norm_w), resident(dmask), resident(zeta),
                  resident(xi), resident(freq2), resident(sign)],
        out_specs=pl.BlockSpec((tm, d), lambda s: (done(s), 0), pipeline_mode=pl.Buffered(1)),
        out_shape=jax.ShapeDtypeStruct((m, d), F32),
        scratch_shapes=[pltpu.VMEM((RET_HEADS, tm, RET_GROUP), F32),
                        pltpu.VMEM((HG_HEADS, tm, HG_GROUP), F32),
                        pltpu.VMEM((RET_HEADS, RET_DK, RET_DV), F32),
                        pltpu.VMEM((HG_HEADS, HG_DV, HG_DK), F32),
                        pltpu.VMEM((tm, MIX_WIDTH), BF16),
                        pltpu.VMEM((tm, d), F32),
                        pltpu.VMEM((tm, d), F32),
                        pltpu.VMEM((tm, d), BF16),
                        pltpu.VMEM((16, 128) if safe else (tm, d), BF16),
                        pltpu.SemaphoreType.DMA(()),
                        pltpu.SemaphoreType.DMA(())],
        compiler_params=pltpu.CompilerParams(dimension_semantics=("arbitrary",),
                                             vmem_limit_bytes=VMEM_LIMIT_BYTES),
        name="fused_layer_safe" if safe else "fused_layer",
    )(x2d, pos3, mod3, mod3, mod3, w_in_slabs, w_out_slabs, ln_w, ln_b,
      lb_logits, hg_norm_w, dmask, zeta, xi, freq2, sign)


def kernel(x, c, positions, w_ada, b_ada, w_in, lb_logits, hg_norm_w, w_out, ln_w, ln_b):
    bsz, seq, d = x.shape
    assert w_ada.shape[0] == DEPTH and seq % RET_CHUNK == 0 and d % PIECE == 0
    mod = _ada_call(c, w_ada[0], b_ada[0])
    mod3 = mod.reshape(bsz, 1, 3 * d)
    operands = (x.reshape(bsz * seq, d), positions.reshape(bsz, seq, 1), mod3, w_in[0], w_out[0],
                ln_w[0].reshape(1, d), ln_b[0].reshape(1, d),
                lb_logits.astype(F32), hg_norm_w[0].reshape(1, HG_WIDTH))

    def run(safe, x2d, pos3, mod3_, w_in_f32, w_out_f32, *rest):
        return _fused_call(x2d, pos3, mod3_, _column_slabs(w_in_f32), _column_slabs(w_out_f32),
                           *rest, seq=seq, safe=safe)

    out = lax.cond(_needs_safe_path(lb_logits),
                   lambda ops: run(True, *ops), lambda ops: run(False, *ops), operands)
    return out.reshape(bsz, seq, d)
```
